```python
import jax, jax.numpy as jnp
from jax import lax
import numpy as np

D_MODEL = 2048
BATCH = 1
SEQ = 16384
DEPTH = 2

N_EVEN = (DEPTH + 1) // 2
N_ODD = DEPTH // 2
EPS = 1e-6
PLE_DIM = 256
A_HEADS = 8
A_HEAD_DIM = 128
A_WIDTH = A_HEADS * A_HEAD_DIM
CHUNK = 128
B_HEADS = 8
Q_LORA = 512
KV_LORA = 256
QK_NOPE = 128
QK_ROPE = 64
V_HEAD = 128
QK_HEAD = QK_NOPE + QK_ROPE
B_WIDTH = B_HEADS * V_HEAD
ROPE_BASE = 10000.0
Q_BLOCK = 128
IN0_WIDTH = Q_LORA + KV_LORA + QK_ROPE + 2 * A_WIDTH
MIX0_WIDTH = B_WIDTH + A_WIDTH
C_WIDTH = D_MODEL
CONV_W = 3
D_FF = -(-8 * D_MODEL // (3 * 256)) * 256

kernel_name = "hybrid_mla_gmlp_shortconv_block"


def rmsnorm(x, g):
    xf = x.astype(jnp.float32)
    y = xf * lax.rsqrt(jnp.mean(xf * xf, axis=-1, keepdims=True) + EPS)
    return y.astype(x.dtype) * g


def layernorm(x, g, b):
    xf = x.astype(jnp.float32)
    mu = jnp.mean(xf, axis=-1, keepdims=True)
    var = jnp.mean(jnp.square(xf - mu), axis=-1, keepdims=True)
    y = (xf - mu) * lax.rsqrt(var + EPS)
    return y.astype(x.dtype) * g + b


def rope_tables(positions):
    half = QK_ROPE // 2
    inv_freq = ROPE_BASE ** (-jnp.arange(half, dtype=jnp.float32) / half)
    ang = positions.astype(jnp.float32)[..., None] * inv_freq
    return jnp.cos(ang), jnp.sin(ang)


def apply_rope(x, cos, sin):
    half = QK_ROPE // 2
    x1, x2 = x[..., :half], x[..., half:]
    cos = cos.astype(x.dtype)
    sin = sin.astype(x.dtype)
    return jnp.concatenate([x1 * cos - x2 * sin, x2 * cos + x1 * sin], axis=-1)


def causal_attention(q, k, v):
    b, s, h, d = q.shape
    nb = s // Q_BLOCK
    scale = d ** -0.5
    qb = q.reshape(b, nb, Q_BLOCK, h, d).transpose(1, 0, 2, 3, 4)
    kpos = jnp.arange(s)

    def one_block(args):
        qi, i = args
        sc = jnp.einsum('bqhd,bkhd->bhqk', qi, k).astype(jnp.float32) * scale
        qpos = i * Q_BLOCK + jnp.arange(Q_BLOCK)
        mask = kpos[None, :] <= qpos[:, None]
        sc = jnp.where(mask, sc, -jnp.inf)
        pr = jax.nn.softmax(sc, axis=-1).astype(v.dtype)
        return jnp.einsum('bhqk,bkhd->bqhd', pr, v)

    o = lax.map(one_block, (qb, jnp.arange(nb)))
    return o.transpose(1, 0, 2, 3, 4).reshape(b, s, h * v.shape[-1])


def mla_mixer(q_lat, kv_lat, k_pe, q_norm, kv_norm, w_qb, w_kvb, cos, sin):
    b, s, _ = q_lat.shape
    q = (rmsnorm(q_lat, q_norm) @ w_qb).reshape(b, s, B_HEADS, QK_HEAD)
    q_nope, q_pe = q[..., :QK_NOPE], q[..., QK_NOPE:]
    q_pe = apply_rope(q_pe, cos[:, :, None, :], sin[:, :, None, :])
    kv = (rmsnorm(kv_lat, kv_norm) @ w_kvb).reshape(b, s, B_HEADS, QK_NOPE + V_HEAD)
    k_nope, v = kv[..., :QK_NOPE], kv[..., QK_NOPE:]
    k_pe = apply_rope(k_pe, cos, sin)
    k = jnp.concatenate([k_nope, jnp.broadcast_to(k_pe[:, :, None, :], (b, s, B_HEADS, QK_ROPE))], axis=-1)
    q = jnp.concatenate([q_nope, q_pe], axis=-1)
    return causal_attention(q, k, v)


def gmlp_mixer(u_pre, v_pre, ln_g, ln_b, w_s, b_s):
    b, s, _ = u_pre.shape
    u = jax.nn.gelu(u_pre)
    v = layernorm(jax.nn.gelu(v_pre), ln_g, ln_b)
    v = v.reshape(b, s // CHUNK, CHUNK, A_HEADS, A_HEAD_DIM)
    w = w_s * jnp.tril(jnp.ones((CHUNK, CHUNK), w_s.dtype))
    g = jnp.einsum('hts,bnshc->bnthc', w, v) + b_s.T[:, :, None]
    return u * g.reshape(b, s, A_WIDTH)


def short_conv(z, w):
    s = z.shape[1]
    zp = jnp.pad(z, ((0, 0), (CONV_W - 1, 0), (0, 0)))
    y = w[0] * zp[:, 0:s]
    for kk in range(1, CONV_W):
        y = y + w[kk] * zp[:, kk:kk + s]
    return y


def setup_inputs(seed: int = 0) -> dict:
    key = jax.random.key(seed)
    ks = jax.random.split(key, 32)
    f32 = jnp.float32

    def w(k, shape, fan_in):
        return jax.random.normal(k, shape, f32) * (fan_in ** -0.5)

    def gain(k, shape):
        return 1.0 + 0.02 * jax.random.normal(k, shape, f32)

    def small(k, shape):
        return 0.02 * jax.random.normal(k, shape, f32)

    return {
        "x": jax.random.normal(ks[0], (BATCH, SEQ, D_MODEL), f32),
        "p": jax.random.normal(ks[1], (DEPTH, BATCH, SEQ, PLE_DIM), f32),
        "positions": jnp.broadcast_to(jnp.arange(SEQ, dtype=jnp.int32), (BATCH, SEQ)),
        "norm_mix": gain(ks[2], (DEPTH, D_MODEL)),
        "norm_ffn": gain(ks[3], (DEPTH, D_MODEL)),
        "norm_ple": gain(ks[4], (DEPTH, D_MODEL)),
        "w_in0": w(ks[5], (N_EVEN, D_MODEL, IN0_WIDTH), D_MODEL),
        "q_norm": gain(ks[6], (N_EVEN, Q_LORA)),
        "kv_norm": gain(ks[7], (N_EVEN, KV_LORA)),
        "w_qb": w(ks[8], (N_EVEN, Q_LORA, B_HEADS * QK_HEAD), Q_LORA),
        "w_kvb": w(ks[9], (N_EVEN, KV_LORA, B_HEADS * (QK_NOPE + V_HEAD)), KV_LORA),
        "v_ln_g": gain(ks[10], (N_EVEN, A_WIDTH)),
        "v_ln_b": small(ks[11], (N_EVEN, A_WIDTH)),
        "w_spatial": w(ks[12], (N_EVEN, A_HEADS, CHUNK, CHUNK), CHUNK),
        "b_spatial": gain(ks[13], (N_EVEN, A_HEADS, CHUNK)),
        "w_out0": w(ks[14], (N_EVEN, MIX0_WIDTH, D_MODEL), MIX0_WIDTH),
        "w_in1": w(ks[15], (N_ODD, D_MODEL, 3 * C_WIDTH), D_MODEL),
        "conv_w": w(ks[16], (N_ODD, CONV_W, C_WIDTH), CONV_W),
        "w_out1": w(ks[17], (N_ODD, C_WIDTH, D_MODEL), C_WIDTH),
        "w_gate": w(ks[18], (DEPTH, D_MODEL, D_FF), D_MODEL),
        "w_up": w(ks[19], (DEPTH, D_MODEL, D_FF), D_MODEL),
        "w_down": w(ks[20], (DEPTH, D_FF, D_MODEL), D_FF),
        "w_ple_gate": w(ks[21], (DEPTH, D_MODEL, D_MODEL), D_MODEL),
        "w_ple_proj": w(ks[22], (DEPTH, PLE_DIM, D_MODEL), PLE_DIM),
        "norm_final": gain(ks[23], (D_MODEL,)),
    }


def reference(x, p, positions, norm_mix, norm_ffn, norm_ple, w_in0, q_norm, kv_norm,
              w_qb, w_kvb, v_ln_g, v_ln_b, w_spatial, b_spatial, w_out0, w_in1, conv_w,
              w_out1, w_gate, w_up, w_down, w_ple_gate, w_ple_proj, norm_final):
    cos, sin = rope_tables(positions)
    o1 = Q_LORA
    o2 = o1 + KV_LORA
    o3 = o2 + QK_ROPE
    o4 = o3 + A_WIDTH
    for i in range(DEPTH):
        j = i // 2
        h = rmsnorm(x, norm_mix[i])
        if i % 2 == 0:
            z = h @ w_in0[j]
            attn = mla_mixer(z[..., :o1], z[..., o1:o2], z[..., o2:o3], q_norm[j], kv_norm[j],
                             w_qb[j], w_kvb[j], cos, sin)
            gm = gmlp_mixer(z[..., o3:o4], z[..., o4:], v_ln_g[j], v_ln_b[j],
                            w_spatial[j], b_spatial[j])
            x = x + jnp.concatenate([attn, gm], axis=-1) @ w_out0[j]
        else:
            z = h @ w_in1[j]
            gb, gc, hv = z[..., :C_WIDTH], z[..., C_WIDTH:2 * C_WIDTH], z[..., 2 * C_WIDTH:]
            y = short_conv(gc * hv, conv_w[j])
            x = x + (gb * y) @ w_out1[j]
        h = rmsnorm(x, norm_ffn[i])
        x = x + (jax.nn.silu(h @ w_gate[i]) * (h @ w_up[i])) @ w_down[i]
        h = rmsnorm(x, norm_ple[i])
        x = x + jax.nn.sigmoid(h @ w_ple_gate[i]) * (p[i] @ w_ple_proj[i])
    return rmsnorm(x, norm_final)
```

```python
import functools
import math

import jax
import jax.numpy as jnp
from jax import lax
from jax.experimental import pallas as pl
from jax.experimental.pallas import tpu as pltpu

F32 = jnp.float32
BF16 = jnp.bfloat16

D_MODEL = 2048
SEQ = 16384
EPS = 1e-6
PLE_DIM = 256
HEADS = 8
HEAD_DIM = 128
CHUNK = 128
Q_LORA = 512
KV_LORA = 256
QK_ROPE = 64
ROPE_HALF = QK_ROPE // 2
QK_HEAD = HEAD_DIM + QK_ROPE
WIDTH = HEADS * HEAD_DIM
ROPE_BASE = 10000.0
D_FF = 5632
CONV_W = 3

VMEM_LIMIT_BYTES = 56 * 1024 * 1024
LANES = 128

ROW_TILE = 512
ATTN_TILE = 512
FF_TILE = 512
Q_SCALE = (QK_HEAD ** -0.5) * math.log2(math.e)
MASK_VALUE = -1e30


def _params(*semantics):
    return pltpu.CompilerParams(dimension_semantics=semantics, vmem_limit_bytes=VMEM_LIMIT_BYTES)


def _const_spec(shape):
    zeros = (0,) * len(shape)
    return pl.BlockSpec(shape, lambda *_: zeros, pipeline_mode=pl.Buffered(1))


def _rms(x, g):
    return x * lax.rsqrt(jnp.mean(x * x, axis=-1, keepdims=True) + EPS) * g


def _dot(a, b):
    return jnp.dot(a, b, preferred_element_type=F32)


def _dot_nt(a, b):
    return lax.dot_general(a, b, (((1,), (1,)), ((), ())), preferred_element_type=F32)


def _mla_pre_kernel(x_ref, pos_ref, g_ref, win_ref, qn_ref, kvn_ref, wqb_ref, wkt_ref, wv_ref,
                    inv_ref, sgn_ref, q_ref, kt_ref, v_ref):
    h = _rms(x_ref[...], g_ref[...]).astype(BF16)
    z = _dot(h, win_ref[...])
    q_lat = z[:, :Q_LORA]
    kv_lat = z[:, Q_LORA:Q_LORA + KV_LORA]
    kk = z[:, Q_LORA + KV_LORA:]

    ang = pos_ref[...].astype(F32) * inv_ref[...]
    cos = jnp.cos(ang)
    sin = jnp.sin(ang) * sgn_ref[...]

    qn = _rms(q_lat, qn_ref[...]).astype(BF16)
    qf = _dot(qn, wqb_ref[...])
    cos4 = jnp.concatenate([cos] * 4, axis=-1)
    sin4 = jnp.concatenate([sin] * 4, axis=-1)
    q_pe = qf[:, WIDTH:WIDTH + 512] * cos4 + qf[:, WIDTH + 512:] * sin4
    for hd in range(HEADS):
        q_ref[hd, :, 0:HEAD_DIM] = (qf[:, hd * HEAD_DIM:(hd + 1) * HEAD_DIM] * Q_SCALE).astype(BF16)
        q_ref[hd, :, HEAD_DIM:QK_HEAD] = (q_pe[:, hd * QK_ROPE:(hd + 1) * QK_ROPE] * Q_SCALE).astype(BF16)

    lane = lax.broadcasted_iota(jnp.int32, ang.shape, 1)
    t = kk * jnp.where(lane < QK_ROPE, cos, sin)
    k_pe = t + pltpu.roll(t, QK_ROPE, axis=1)
    k_pe_t = k_pe.T[:QK_ROPE, :].astype(BF16)

    kvn = _rms(kv_lat, kvn_ref[...]).astype(BF16)
    kn_t = _dot_nt(wkt_ref[...], kvn)
    vv = _dot(kvn, wv_ref[...])
    for hd in range(HEADS):
        kt_ref[hd, 0, 0:HEAD_DIM, :] = kn_t[hd * HEAD_DIM:(hd + 1) * HEAD_DIM, :].astype(BF16)
        kt_ref[hd, 0, HEAD_DIM:QK_HEAD, :] = k_pe_t
        v_ref[hd] = vv[:, hd * HEAD_DIM:(hd + 1) * HEAD_DIM].astype(BF16)


def _mla_pre(x, pos, g, win, qn, kvn, wqb, wkt, wv, inv4, sgn4):
    tm = ATTN_TILE
    n = SEQ // tm
    return pl.pallas_call(
        _mla_pre_kernel,
        grid=(n,),
        in_specs=[
            pl.BlockSpec((tm, D_MODEL), lambda i: (i, 0)),
            pl.BlockSpec((tm, 1), lambda i: (i, 0)),
            _const_spec(g.shape), _const_spec(win.shape), _const_spec(qn.shape), _const_spec(kvn.shape),
            _const_spec(wqb.shape), _const_spec(wkt.shape), _const_spec(wv.shape),
            _const_spec(inv4.shape), _const_spec(sgn4.shape),
        ],
        out_specs=[
            pl.BlockSpec((HEADS, tm, QK_HEAD), lambda i: (0, i, 0)),
            pl.BlockSpec((HEADS, 1, QK_HEAD, tm), lambda i: (0, i, 0, 0)),
            pl.BlockSpec((HEADS, tm, HEAD_DIM), lambda i: (0, i, 0)),
        ],
        out_shape=[
            jax.ShapeDtypeStruct((HEADS, SEQ, QK_HEAD), BF16),
            jax.ShapeDtypeStruct((HEADS, n, QK_HEAD, tm), BF16),
            jax.ShapeDtypeStruct((HEADS, SEQ, HEAD_DIM), BF16),
        ],
        compiler_params=_params("arbitrary"),
        name="mla_pre",
    )(x, pos, g, win, qn, kvn, wqb, wkt, wv, inv4, sgn4)


def _flash_kernel(q_ref, kt_ref, v_ref, o_ref, m_sc, l_sc, acc_sc):
    qi = pl.program_id(1)
    tq = ATTN_TILE
    q = q_ref[0]
    m_sc[...] = jnp.full(m_sc.shape, MASK_VALUE, F32)
    l_sc[...] = jnp.zeros(l_sc.shape, F32)
    acc_sc[...] = jnp.zeros(acc_sc.shape, F32)

    def step(j, masked):
        s = _dot(q, kt_ref[0, j])
        if masked:
            row = lax.broadcasted_iota(jnp.int32, s.shape, 0)
            col = lax.broadcasted_iota(jnp.int32, s.shape, 1)
            s = jnp.where(col <= row, s, MASK_VALUE)
        m_prev = m_sc[...]
        m_new = jnp.maximum(m_prev, jnp.max(s, axis=-1, keepdims=True))
        alpha = jnp.exp2(m_prev - m_new)
        p = jnp.exp2(s - m_new)
        l_sc[...] = alpha * l_sc[...] + jnp.sum(p, axis=-1, keepdims=True)
        start = pl.multiple_of(j * tq, tq)
        acc_sc[...] = alpha * acc_sc[...] + _dot(p.astype(BF16), v_ref[0, pl.ds(start, tq), :])
        m_sc[...] = m_new

    def body(j, carry):
        step(j, masked=False)
        return carry

    lax.fori_loop(0, qi, body, 0)
    step(qi, masked=True)
    o_ref[...] = (acc_sc[...] / l_sc[...]).astype(BF16)


def _flash(q, kt, v):
    tq = ATTN_TILE
    n = SEQ // tq
    return pl.pallas_call(
        _flash_kernel,
        grid=(HEADS, n),
        in_specs=[
            pl.BlockSpec((1, tq, QK_HEAD), lambda h, i: (h, i, 0)),
            pl.BlockSpec((1, n, QK_HEAD, tq), lambda h, i: (h, 0, 0, 0)),
            pl.BlockSpec((1, SEQ, HEAD_DIM), lambda h, i: (h, 0, 0)),
        ],
        out_specs=pl.BlockSpec((tq, HEAD_DIM), lambda h, i: (i, h)),
        out_shape=jax.ShapeDtypeStruct((SEQ, WIDTH), BF16),
        scratch_shapes=[
            pltpu.VMEM((tq, 1), F32),
            pltpu.VMEM((tq, 1), F32),
            pltpu.VMEM((tq, HEAD_DIM), F32),
        ],
        compiler_params=_params("arbitrary", "arbitrary"),
        name="mla_flash",
    )(q, kt, v)


def _gelu(x):
    return jax.nn.gelu(x)


def _gmlp_kernel(x_ref, g_ref, wu_ref, wv_ref, lng_ref, lnb_ref, ws_ref, bst_ref, o_ref):
    tm = x_ref.shape[0]
    h = _rms(x_ref[...], g_ref[...]).astype(BF16)
    u = _gelu(_dot(h, wu_ref[...]))
    gv = _gelu(_dot(h, wv_ref[...]))
    mu = jnp.mean(gv, axis=-1, keepdims=True)
    var = jnp.mean(jnp.square(gv - mu), axis=-1, keepdims=True)
    vn = ((gv - mu) * lax.rsqrt(var + EPS) * lng_ref[...] + lnb_ref[...]).astype(BF16)
    row = lax.broadcasted_iota(jnp.int32, (CHUNK, CHUNK), 0)
    col = lax.broadcasted_iota(jnp.int32, (CHUNK, CHUNK), 1)
    bst = bst_ref[...]
    for hd in range(HEADS):
        w = jnp.where(col <= row, ws_ref[hd], 0.0).astype(BF16)
        bias = bst[:, hd:hd + 1]
        cs = slice(hd * HEAD_DIM, (hd + 1) * HEAD_DIM)
        for c in range(tm // CHUNK):
            rs = slice(c * CHUNK, (c + 1) * CHUNK)
            gate = _dot(w, vn[rs, cs]) + bias
            o_ref[rs, cs] = (u[rs, cs] * gate).astype(BF16)


def _gmlp(x, g, wu, wv, lng, lnb, ws, bst):
    tm = ROW_TILE
    return pl.pallas_call(
        _gmlp_kernel,
        grid=(SEQ // tm,),
        in_specs=[
            pl.BlockSpec((tm, D_MODEL), lambda i: (i, 0)),
            _const_spec(g.shape), _const_spec(wu.shape), _const_spec(wv.shape),
            _const_spec(lng.shape), _const_spec(lnb.shape), _const_spec(ws.shape), _const_spec(bst.shape),
        ],
        out_specs=pl.BlockSpec((tm, WIDTH), lambda i: (i, 0)),
        out_shape=jax.ShapeDtypeStruct((SEQ, WIDTH), BF16),
        compiler_params=_params("arbitrary"),
        name="gmlp",
    )(x, g, wu, wv, lng, lnb, ws, bst)


def _outproj_kernel(x_ref, a_ref, b_ref, wa_ref, wb_ref, o_ref):
    o_ref[...] = x_ref[...] + _dot(a_ref[...], wa_ref[...]) + _dot(b_ref[...], wb_ref[...])


def _outproj(x, a, b, wa, wb):
    tm = ROW_TILE
    return pl.pallas_call(
        _outproj_kernel,
        grid=(SEQ // tm,),
        in_specs=[
            pl.BlockSpec((tm, D_MODEL), lambda i: (i, 0)),
            pl.BlockSpec((tm, WIDTH), lambda i: (i, 0)),
            pl.BlockSpec((tm, WIDTH), lambda i: (i, 0)),
            _const_spec(wa.shape), _const_spec(wb.shape),
        ],
        out_specs=pl.BlockSpec((tm, D_MODEL), lambda i: (i, 0)),
        out_shape=jax.ShapeDtypeStruct((SEQ, D_MODEL), F32),
        compiler_params=_params("arbitrary"),
        name="outproj0",
    )(x, a, b, wa, wb)


def _ffn_kernel(x_ref, g_ref, wg_ref, wu_ref, wd_ref, o_ref, h_sc):
    @pl.when(pl.program_id(1) == 0)
    def _():
        x = x_ref[...]
        h_sc[...] = _rms(x, g_ref[...]).astype(BF16)
        o_ref[...] = x

    h = h_sc[...]
    a = _dot(h, wg_ref[...])
    b = _dot(h, wu_ref[...])
    t = (a * jax.nn.sigmoid(a) * b).astype(BF16)
    o_ref[...] += _dot(t, wd_ref[...])


def _ffn(x, g, wg, wu, wd):
    tm, tf = ROW_TILE, FF_TILE
    return pl.pallas_call(
        _ffn_kernel,
        grid=(SEQ // tm, D_FF // tf),
        in_specs=[
            pl.BlockSpec((tm, D_MODEL), lambda i, j: (i, 0)),
            _const_spec(g.shape),
            pl.BlockSpec((D_MODEL, tf), lambda i, j: (0, j)),
            pl.BlockSpec((D_MODEL, tf), lambda i, j: (0, j)),
            pl.BlockSpec((tf, D_MODEL), lambda i, j: (j, 0)),
        ],
        out_specs=pl.BlockSpec((tm, D_MODEL), lambda i, j: (i, 0)),
        out_shape=jax.ShapeDtypeStruct((SEQ, D_MODEL), F32),
        scratch_shapes=[pltpu.VMEM((tm, D_MODEL), BF16)],
        compiler_params=_params("arbitrary", "arbitrary"),
        name="ffn",
    )(x, g, wg, wu, wd)


def _conv_kernel(x_ref, g_ref, wb_ref, wc_ref, wh_ref, cw_ref, wo_ref, o_ref, h_sc, tail_sc):
    i = pl.program_id(0)
    j = pl.program_id(1)

    @pl.when(j == 0)
    def _():
        x = x_ref[...]
        h_sc[...] = _rms(x, g_ref[...]).astype(BF16)
        o_ref[...] = x

    @pl.when(i == 0)
    def _():
        tail_sc[j] = jnp.zeros(tail_sc.shape[1:], F32)

    h = h_sc[...]
    gb = _dot(h, wb_ref[...])
    cz = _dot(h, wc_ref[...]) * _dot(h, wh_ref[...])
    tail = tail_sc[j]
    prev1 = tail[7:8, :]
    prev2 = tail[6:7, :]
    row = lax.broadcasted_iota(jnp.int32, cz.shape, 0)
    z1 = jnp.where(row == 0, prev1, pltpu.roll(cz, 1, axis=0))
    z2 = jnp.where(row == 0, prev2, jnp.where(row == 1, prev1, pltpu.roll(cz, 2, axis=0)))
    cw = cw_ref[...]
    y = cw[0:1, :] * z2 + cw[1:2, :] * z1 + cw[2:3, :] * cz
    tail_sc[j] = cz[cz.shape[0] - 8:, :]
    o_ref[...] += _dot((gb * y).astype(BF16), wo_ref[...])


def _conv_mixer(x, g, win, cw, wo):
    tm, tc = ROW_TILE, FF_TILE
    nc = D_MODEL // tc
    return pl.pallas_call(
        _conv_kernel,
        grid=(SEQ // tm, nc),
        in_specs=[
            pl.BlockSpec((tm, D_MODEL), lambda i, j: (i, 0)),
            _const_spec(g.shape),
            pl.BlockSpec((D_MODEL, tc), lambda i, j: (0, j)),
            pl.BlockSpec((D_MODEL, tc), lambda i, j: (0, j + nc)),
            pl.BlockSpec((D_MODEL, tc), lambda i, j: (0, j + 2 * nc)),
            pl.BlockSpec((8, tc), lambda i, j: (0, j)),
            pl.BlockSpec((tc, D_MODEL), lambda i, j: (j, 0)),
        ],
        out_specs=pl.BlockSpec((tm, D_MODEL), lambda i, j: (i, 0)),
        out_shape=jax.ShapeDtypeStruct((SEQ, D_MODEL), F32),
        scratch_shapes=[pltpu.VMEM((tm, D_MODEL), BF16), pltpu.VMEM((nc, 8, tc), F32)],
        compiler_params=_params("arbitrary", "arbitrary"),
        name="conv_mixer",
    )(x, g, win, win, win, cw, wo)


def _ple_kernel(x_ref, p_ref, g_ref, wg_ref, wp_ref, gf_ref, o_ref, *, final_norm):
    x = x_ref[...]
    h = _rms(x, g_ref[...]).astype(BF16)
    gate = jax.nn.sigmoid(_dot(h, wg_ref[...]))
    y = x + gate * _dot(p_ref[...].astype(BF16), wp_ref[...])
    if final_norm:
        y = _rms(y, gf_ref[...])
    o_ref[...] = y


def _ple(x, p, g, wg, wp, gf, final_norm):
    tm = ROW_TILE
    return pl.pallas_call(
        functools.partial(_ple_kernel, final_norm=final_norm),
        grid=(SEQ // tm,),
        in_specs=[
            pl.BlockSpec((tm, D_MODEL), lambda i: (i, 0)),
            pl.BlockSpec((tm, PLE_DIM), lambda i: (i, 0)),
            _const_spec(g.shape), _const_spec(wg.shape), _const_spec(wp.shape), _const_spec(gf.shape),
        ],
        out_specs=pl.BlockSpec((tm, D_MODEL), lambda i: (i, 0)),
        out_shape=jax.ShapeDtypeStruct((SEQ, D_MODEL), F32),
        compiler_params=_params("arbitrary"),
        name="ple",
    )(x, p, g, wg, wp, gf)


def _row(v):
    return v.reshape(1, -1)


def kernel(x, p, positions, norm_mix, norm_ffn, norm_ple, w_in0, q_norm, kv_norm, w_qb, w_kvb, v_ln_g,
           v_ln_b, w_spatial, b_spatial, w_out0, w_in1, conv_w, w_out1, w_gate, w_up, w_down, w_ple_gate,
           w_ple_proj, norm_final):
    assert x.shape == (1, SEQ, D_MODEL) and p.shape == (2, 1, SEQ, PLE_DIM)
    xs = x.reshape(SEQ, D_MODEL)
    pos = positions.reshape(SEQ, 1)

    o1, o2, o3 = Q_LORA, Q_LORA + KV_LORA, Q_LORA + KV_LORA + QK_ROPE
    w0 = w_in0[0]
    k1, k2 = w0[:, o2:o2 + ROPE_HALF], w0[:, o2 + ROPE_HALF:o3]
    win_mla = jnp.concatenate([w0[:, :o2], k1, k2, k2, k1], axis=1).astype(BF16)
    wu0 = w0[:, o3:o3 + WIDTH].astype(BF16)
    wv0 = w0[:, o3 + WIDTH:].astype(BF16)
    wq = w_qb[0].reshape(Q_LORA, HEADS, QK_HEAD)
    pe = wq[:, :, HEAD_DIM:]
    pe_swapped = jnp.concatenate([pe[:, :, ROPE_HALF:], pe[:, :, :ROPE_HALF]], axis=-1)
    wqb = jnp.concatenate([wq[:, :, :HEAD_DIM].reshape(Q_LORA, WIDTH), pe.reshape(Q_LORA, HEADS * QK_ROPE),
                           pe_swapped.reshape(Q_LORA, HEADS * QK_ROPE)], axis=1).astype(BF16)
    wkv = w_kvb[0].reshape(KV_LORA, HEADS, 2 * HEAD_DIM)
    wkt = wkv[:, :, :HEAD_DIM].reshape(KV_LORA, WIDTH).T.astype(BF16)
    wv = wkv[:, :, HEAD_DIM:].reshape(KV_LORA, WIDTH).astype(BF16)
    inv_freq = ROPE_BASE ** (-jnp.arange(ROPE_HALF, dtype=F32) / ROPE_HALF)
    inv4 = jnp.tile(inv_freq, LANES // ROPE_HALF).reshape(1, LANES)
    sgn4 = jnp.tile(jnp.concatenate([-jnp.ones(ROPE_HALF, F32), jnp.ones(ROPE_HALF, F32)]), 2).reshape(1, LANES)
    wo_attn = w_out0[0, :WIDTH].astype(BF16)
    wo_gmlp = w_out0[0, WIDTH:].astype(BF16)
    cw = jnp.concatenate([conv_w[0], jnp.zeros((8 - CONV_W, D_MODEL), F32)], axis=0)

    q, kt, v = _mla_pre(xs, pos, _row(norm_mix[0]), win_mla, _row(q_norm[0]), _row(kv_norm[0]), wqb, wkt, wv,
                        inv4, sgn4)
    attn = _flash(q, kt, v)
    gm = _gmlp(xs, _row(norm_mix[0]), wu0, wv0, _row(v_ln_g[0]), _row(v_ln_b[0]), w_spatial[0],
               b_spatial[0].T)
    xs = _outproj(xs, attn, gm, wo_attn, wo_gmlp)
    xs = _ffn(xs, _row(norm_ffn[0]), w_gate[0].astype(BF16), w_up[0].astype(BF16), w_down[0].astype(BF16))
    xs = _ple(xs, p[0, 0], _row(norm_ple[0]), w_ple_gate[0].astype(BF16), w_ple_proj[0].astype(BF16),
              _row(norm_final), final_norm=False)

    xs = _conv_mixer(xs, _row(norm_mix[1]), w_in1[0].astype(BF16), cw, w_out1[0].astype(BF16))
    xs = _ffn(xs, _row(norm_ffn[1]), w_gate[1].astype(BF16), w_up[1].astype(BF16), w_down[1].astype(BF16))
    xs = _ple(xs, p[1, 0], _row(norm_ple[1]), w_ple_gate[1].astype(BF16), w_ple_proj[1].astype(BF16),
              _row(norm_final), final_norm=True)
    return xs.reshape(1, SEQ, D_MODEL)
```

```python
import functools
import math

import jax
import jax.numpy as jnp
from jax import lax
from jax.experimental import pallas as pl
from jax.experimental.pallas import tpu as pltpu

F32 = jnp.float32
BF16 = jnp.bfloat16

D_MODEL = 2048
SEQ = 16384
EPS = 1e-6
PLE_DIM = 256
HEADS = 8
HEAD_DIM = 128
CHUNK = 128
Q_LORA = 512
KV_LORA = 256
QK_ROPE = 64
ROPE_HALF = QK_ROPE // 2
QK_HEAD = HEAD_DIM + QK_ROPE
WIDTH = HEADS * HEAD_DIM
ROPE_BASE = 10000.0
D_FF = 5632
CONV_W = 3

VMEM_LIMIT_BYTES = 56 * 1024 * 1024
LANES = 128

ROW_TILE = 512
ATTN_TILE = 512
HEAD_GROUP = 4
FF_TILE = 512
Q_SCALE = (QK_HEAD ** -0.5) * math.log2(math.e)
MASK_VALUE = -1e30


def _params(*semantics):
    return pltpu.CompilerParams(dimension_semantics=semantics, vmem_limit_bytes=VMEM_LIMIT_BYTES)


def _const_spec(shape):
    zeros = (0,) * len(shape)
    return pl.BlockSpec(shape, lambda *_: zeros, pipeline_mode=pl.Buffered(1))


def _rms(x, g):
    return x * lax.rsqrt(jnp.mean(x * x, axis=-1, keepdims=True) + EPS) * g


def _dot(a, b):
    return jnp.dot(a, b, preferred_element_type=F32)


def _dot_nt(a, b):
    return lax.dot_general(a, b, (((1,), (1,)), ((), ())), preferred_element_type=F32)


def _mla_pre_kernel(x_ref, pos_ref, g_ref, win_ref, qn_ref, kvn_ref, wqb_ref, wkt_ref, wv_ref,
                    inv_ref, sgn_ref, q_ref, kt_ref, v_ref):
    h = _rms(x_ref[...], g_ref[...]).astype(BF16)
    z = _dot(h, win_ref[...])
    q_lat = z[:, :Q_LORA]
    kv_lat = z[:, Q_LORA:Q_LORA + KV_LORA]
    kk = z[:, Q_LORA + KV_LORA:]

    ang = pos_ref[...].astype(F32) * inv_ref[...]
    cos = jnp.cos(ang)
    sin = jnp.sin(ang) * sgn_ref[...]

    qn = _rms(q_lat, qn_ref[...]).astype(BF16)
    qf = _dot(qn, wqb_ref[...])
    cos4 = jnp.concatenate([cos] * 4, axis=-1)
    sin4 = jnp.concatenate([sin] * 4, axis=-1)
    q_pe = qf[:, WIDTH:WIDTH + 512] * cos4 + qf[:, WIDTH + 512:] * sin4
    for hd in range(HEADS):
        q_ref[hd, :, 0:HEAD_DIM] = (qf[:, hd * HEAD_DIM:(hd + 1) * HEAD_DIM] * Q_SCALE).astype(BF16)
        q_ref[hd, :, HEAD_DIM:QK_HEAD] = (q_pe[:, hd * QK_ROPE:(hd + 1) * QK_ROPE] * Q_SCALE).astype(BF16)

    lane = lax.broadcasted_iota(jnp.int32, ang.shape, 1)
    t = kk * jnp.where(lane < QK_ROPE, cos, sin)
    k_pe = t + pltpu.roll(t, QK_ROPE, axis=1)
    k_pe_t = k_pe.T[:QK_ROPE, :].astype(BF16)

    kvn = _rms(kv_lat, kvn_ref[...]).astype(BF16)
    kn_t = _dot_nt(wkt_ref[...], kvn)
    vv = _dot(kvn, wv_ref[...])
    for hd in range(HEADS):
        kt_ref[hd, 0, 0:HEAD_DIM, :] = kn_t[hd * HEAD_DIM:(hd + 1) * HEAD_DIM, :].astype(BF16)
        kt_ref[hd, 0, HEAD_DIM:QK_HEAD, :] = k_pe_t
        v_ref[hd] = vv[:, hd * HEAD_DIM:(hd + 1) * HEAD_DIM].astype(BF16)


def _mla_pre(x, pos, g, win, qn, kvn, wqb, wkt, wv, inv4, sgn4):
    tm = ATTN_TILE
    n = SEQ // tm
    return pl.pallas_call(
        _mla_pre_kernel,
        grid=(n,),
        in_specs=[
            pl.BlockSpec((tm, D_MODEL), lambda i: (i, 0)),
            pl.BlockSpec((tm, 1), lambda i: (i, 0)),
            _const_spec(g.shape), _const_spec(win.shape), _const_spec(qn.shape), _const_spec(kvn.shape),
            _const_spec(wqb.shape), _const_spec(wkt.shape), _const_spec(wv.shape),
            _const_spec(inv4.shape), _const_spec(sgn4.shape),
        ],
        out_specs=[
            pl.BlockSpec((HEADS, tm, QK_HEAD), lambda i: (0, i, 0)),
            pl.BlockSpec((HEADS, 1, QK_HEAD, tm), lambda i: (0, i, 0, 0)),
            pl.BlockSpec((HEADS, tm, HEAD_DIM), lambda i: (0, i, 0)),
        ],
        out_shape=[
            jax.ShapeDtypeStruct((HEADS, SEQ, QK_HEAD), BF16),
            jax.ShapeDtypeStruct((HEADS, n, QK_HEAD, tm), BF16),
            jax.ShapeDtypeStruct((HEADS, SEQ, HEAD_DIM), BF16),
        ],
        compiler_params=_params("arbitrary"),
        name="mla_pre",
    )(x, pos, g, win, qn, kvn, wqb, wkt, wv, inv4, sgn4)


def _flash_kernel(q_ref, kt_ref, v_ref, o_ref, m_sc, l_sc, acc_sc):
    qi = pl.program_id(1)
    tq = ATTN_TILE
    ncol = tq // LANES
    m_sc[...] = jnp.full(m_sc.shape, MASK_VALUE, F32)
    l_sc[...] = jnp.zeros(l_sc.shape, F32)
    acc_sc[...] = jnp.zeros(acc_sc.shape, F32)

    def step(j, masked):
        start = pl.multiple_of(j * tq, tq)
        for g in range(HEAD_GROUP):
            s = _dot(q_ref[g], kt_ref[g, j])
            if masked:
                row = lax.broadcasted_iota(jnp.int32, s.shape, 0)
                col = lax.broadcasted_iota(jnp.int32, s.shape, 1)
                s = jnp.where(col <= row, s, MASK_VALUE)
            cols = [s[:, c * LANES:(c + 1) * LANES] for c in range(ncol)]
            m_prev = m_sc[g]
            m_cur = functools.reduce(jnp.maximum, cols)
            m_new = jnp.maximum(m_prev, jnp.max(m_cur, axis=-1, keepdims=True))
            alpha = jnp.exp2(m_prev - m_new)
            ps = [jnp.exp2(c - m_new) for c in cols]
            l_sc[g] = alpha * l_sc[g] + functools.reduce(jnp.add, ps)
            p = jnp.concatenate(ps, axis=-1).astype(BF16)
            acc_sc[g] = alpha * acc_sc[g] + _dot(p, v_ref[g, pl.ds(start, tq), :])
            m_sc[g] = m_new

    def body(j, carry):
        step(j, masked=False)
        return carry

    lax.fori_loop(0, qi, body, 0)
    step(qi, masked=True)
    for g in range(HEAD_GROUP):
        l = jnp.sum(l_sc[g], axis=-1, keepdims=True)
        o_ref[:, g * HEAD_DIM:(g + 1) * HEAD_DIM] = (acc_sc[g] / l).astype(BF16)


def _flash(q, kt, v):
    tq = ATTN_TILE
    n = SEQ // tq
    hg = HEAD_GROUP
    return pl.pallas_call(
        _flash_kernel,
        grid=(HEADS // hg, n),
        in_specs=[
            pl.BlockSpec((hg, tq, QK_HEAD), lambda h, i: (h, i, 0)),
            pl.BlockSpec((hg, n, QK_HEAD, tq), lambda h, i: (h, 0, 0, 0), pipeline_mode=pl.Buffered(1)),
            pl.BlockSpec((hg, SEQ, HEAD_DIM), lambda h, i: (h, 0, 0), pipeline_mode=pl.Buffered(1)),
        ],
        out_specs=pl.BlockSpec((tq, hg * HEAD_DIM), lambda h, i: (i, h)),
        out_shape=jax.ShapeDtypeStruct((SEQ, WIDTH), BF16),
        scratch_shapes=[
            pltpu.VMEM((hg, tq, LANES), F32),
            pltpu.VMEM((hg, tq, LANES), F32),
            pltpu.VMEM((hg, tq, HEAD_DIM), F32),
        ],
        compiler_params=_params("arbitrary", "arbitrary"),
        name="mla_flash",
    )(q, kt, v)


def _gelu(x):
    return jax.nn.gelu(x)


def _gmlp_kernel(x_ref, g_ref, wu_ref, wv_ref, lng_ref, lnb_ref, ws_ref, bst_ref, o_ref):
    tm = x_ref.shape[0]
    h = _rms(x_ref[...], g_ref[...]).astype(BF16)
    u = _gelu(_dot(h, wu_ref[...]))
    gv = _gelu(_dot(h, wv_ref[...]))
    mu = jnp.mean(gv, axis=-1, keepdims=True)
    var = jnp.mean(jnp.square(gv - mu), axis=-1, keepdims=True)
    vn = ((gv - mu) * lax.rsqrt(var + EPS) * lng_ref[...] + lnb_ref[...]).astype(BF16)
    row = lax.broadcasted_iota(jnp.int32, (CHUNK, CHUNK), 0)
    col = lax.broadcasted_iota(jnp.int32, (CHUNK, CHUNK), 1)
    bst = bst_ref[...]
    for hd in range(HEADS):
        w = jnp.where(col <= row, ws_ref[hd], 0.0).astype(BF16)
        bias = bst[:, hd:hd + 1]
        cs = slice(hd * HEAD_DIM, (hd + 1) * HEAD_DIM)
        for c in range(tm // CHUNK):
            rs = slice(c * CHUNK, (c + 1) * CHUNK)
            gate = _dot(w, vn[rs, cs]) + bias
            o_ref[rs, cs] = (u[rs, cs] * gate).astype(BF16)


def _gmlp(x, g, wu, wv, lng, lnb, ws, bst):
    tm = ROW_TILE
    return pl.pallas_call(
        _gmlp_kernel,
        grid=(SEQ // tm,),
        in_specs=[
            pl.BlockSpec((tm, D_MODEL), lambda i: (i, 0)),
            _const_spec(g.shape), _const_spec(wu.shape), _const_spec(wv.shape),
            _const_spec(lng.shape), _const_spec(lnb.shape), _const_spec(ws.shape), _const_spec(bst.shape),
        ],
        out_specs=pl.BlockSpec((tm, WIDTH), lambda i: (i, 0)),
        out_shape=jax.ShapeDtypeStruct((SEQ, WIDTH), BF16),
        compiler_params=_params("arbitrary"),
        name="gmlp",
    )(x, g, wu, wv, lng, lnb, ws, bst)


def _outproj_kernel(x_ref, a_ref, b_ref, wa_ref, wb_ref, o_ref):
    o_ref[...] = x_ref[...] + _dot(a_ref[...], wa_ref[...]) + _dot(b_ref[...], wb_ref[...])


def _outproj(x, a, b, wa, wb):
    tm = ROW_TILE
    return pl.pallas_call(
        _outproj_kernel,
        grid=(SEQ // tm,),
        in_specs=[
            pl.BlockSpec((tm, D_MODEL), lambda i: (i, 0)),
            pl.BlockSpec((tm, WIDTH), lambda i: (i, 0)),
            pl.BlockSpec((tm, WIDTH), lambda i: (i, 0)),
            _const_spec(wa.shape), _const_spec(wb.shape),
        ],
        out_specs=pl.BlockSpec((tm, D_MODEL), lambda i: (i, 0)),
        out_shape=jax.ShapeDtypeStruct((SEQ, D_MODEL), F32),
        compiler_params=_params("arbitrary"),
        name="outproj0",
    )(x, a, b, wa, wb)


def _ffn_kernel(x_ref, g_ref, wg_ref, wu_ref, wd_ref, o_ref, h_sc):
    @pl.when(pl.program_id(1) == 0)
    def _():
        x = x_ref[...]
        h_sc[...] = _rms(x, g_ref[...]).astype(BF16)
        o_ref[...] = x

    h = h_sc[...]
    a = _dot(h, wg_ref[...])
    b = _dot(h, wu_ref[...])
    t = (a * jax.nn.sigmoid(a) * b).astype(BF16)
    o_ref[...] += _dot(t, wd_ref[...])


def _ffn(x, g, wg, wu, wd):
    tm, tf = ROW_TILE, FF_TILE
    return pl.pallas_call(
        _ffn_kernel,
        grid=(SEQ // tm, D_FF // tf),
        in_specs=[
            pl.BlockSpec((tm, D_MODEL), lambda i, j: (i, 0)),
            _const_spec(g.shape),
            pl.BlockSpec((D_MODEL, tf), lambda i, j: (0, j)),
            pl.BlockSpec((D_MODEL, tf), lambda i, j: (0, j)),
            pl.BlockSpec((tf, D_MODEL), lambda i, j: (j, 0)),
        ],
        out_specs=pl.BlockSpec((tm, D_MODEL), lambda i, j: (i, 0)),
        out_shape=jax.ShapeDtypeStruct((SEQ, D_MODEL), F32),
        scratch_shapes=[pltpu.VMEM((tm, D_MODEL), BF16)],
        compiler_params=_params("arbitrary", "arbitrary"),
        name="ffn",
    )(x, g, wg, wu, wd)


def _conv_kernel(x_ref, g_ref, wb_ref, wc_ref, wh_ref, cw_ref, wo_ref, o_ref, h_sc, tail_sc):
    i = pl.program_id(0)
    j = pl.program_id(1)

    @pl.when(j == 0)
    def _():
        x = x_ref[...]
        h_sc[...] = _rms(x, g_ref[...]).astype(BF16)
        o_ref[...] = x

    @pl.when(i == 0)
    def _():
        tail_sc[j] = jnp.zeros(tail_sc.shape[1:], F32)

    h = h_sc[...]
    gb = _dot(h, wb_ref[...])
    cz = _dot(h, wc_ref[...]) * _dot(h, wh_ref[...])
    tail = tail_sc[j]
    prev1 = tail[7:8, :]
    prev2 = tail[6:7, :]
    row = lax.broadcasted_iota(jnp.int32, cz.shape, 0)
    z1 = jnp.where(row == 0, prev1, pltpu.roll(cz, 1, axis=0))
    z2 = jnp.where(row == 0, prev2, jnp.where(row == 1, prev1, pltpu.roll(cz, 2, axis=0)))
    cw = cw_ref[...]
    y = cw[0:1, :] * z2 + cw[1:2, :] * z1 + cw[2:3, :] * cz
    tail_sc[j] = cz[cz.shape[0] - 8:, :]
    o_ref[...] += _dot((gb * y).astype(BF16), wo_ref[...])


def _conv_mixer(x, g, win, cw, wo):
    tm, tc = ROW_TILE, FF_TILE
    nc = D_MODEL // tc
    return pl.pallas_call(
        _conv_kernel,
        grid=(SEQ // tm, nc),
        in_specs=[
            pl.BlockSpec((tm, D_MODEL), lambda i, j: (i, 0)),
            _const_spec(g.shape),
            pl.BlockSpec((D_MODEL, tc), lambda i, j: (0, j)),
            pl.BlockSpec((D_MODEL, tc), lambda i, j: (0, j + nc)),
            pl.BlockSpec((D_MODEL, tc), lambda i, j: (0, j + 2 * nc)),
            pl.BlockSpec((8, tc), lambda i, j: (0, j)),
            pl.BlockSpec((tc, D_MODEL), lambda i, j: (j, 0)),
        ],
        out_specs=pl.BlockSpec((tm, D_MODEL), lambda i, j: (i, 0)),
        out_shape=jax.ShapeDtypeStruct((SEQ, D_MODEL), F32),
        scratch_shapes=[pltpu.VMEM((tm, D_MODEL), BF16), pltpu.VMEM((nc, 8, tc), F32)],
        compiler_params=_params("arbitrary", "arbitrary"),
        name="conv_mixer",
    )(x, g, win, win, win, cw, wo)


def _ple_kernel(x_ref, p_ref, g_ref, wg_ref, wp_ref, gf_ref, o_ref, *, final_norm):
    x = x_ref[...]
    h = _rms(x, g_ref[...]).astype(BF16)
    gate = jax.nn.sigmoid(_dot(h, wg_ref[...]))
    y = x + gate * _dot(p_ref[...].astype(BF16), wp_ref[...])
    if final_norm:
        y = _rms(y, gf_ref[...])
    o_ref[...] = y


def _ple(x, p, g, wg, wp, gf, final_norm):
    tm = ROW_TILE
    return pl.pallas_call(
        functools.partial(_ple_kernel, final_norm=final_norm),
        grid=(SEQ // tm,),
        in_specs=[
            pl.BlockSpec((tm, D_MODEL), lambda i: (i, 0)),
            pl.BlockSpec((tm, PLE_DIM), lambda i: (i, 0)),
            _const_spec(g.shape), _const_spec(wg.shape), _const_spec(wp.shape), _const_spec(gf.shape),
        ],
        out_specs=pl.BlockSpec((tm, D_MODEL), lambda i: (i, 0)),
        out_shape=jax.ShapeDtypeStruct((SEQ, D_MODEL), F32),
        compiler_params=_params("arbitrary"),
        name="ple",
    )(x, p, g, wg, wp, gf)


def _row(v):
    return v.reshape(1, -1)


def kernel(x, p, positions, norm_mix, norm_ffn, norm_ple, w_in0, q_norm, kv_norm, w_qb, w_kvb, v_ln_g,
           v_ln_b, w_spatial, b_spatial, w_out0, w_in1, conv_w, w_out1, w_gate, w_up, w_down, w_ple_gate,
           w_ple_proj, norm_final):
    assert x.shape == (1, SEQ, D_MODEL) and p.shape == (2, 1, SEQ, PLE_DIM)
    xs = x.reshape(SEQ, D_MODEL)
    pos = positions.reshape(SEQ, 1)

    o1, o2, o3 = Q_LORA, Q_LORA + KV_LORA, Q_LORA + KV_LORA + QK_ROPE
    w0 = w_in0[0]
    k1, k2 = w0[:, o2:o2 + ROPE_HALF], w0[:, o2 + ROPE_HALF:o3]
    win_mla = jnp.concatenate([w0[:, :o2], k1, k2, k2, k1], axis=1).astype(BF16)
    wu0 = w0[:, o3:o3 + WIDTH].astype(BF16)
    wv0 = w0[:, o3 + WIDTH:].astype(BF16)
    wq = w_qb[0].reshape(Q_LORA, HEADS, QK_HEAD)
    pe = wq[:, :, HEAD_DIM:]
    pe_swapped = jnp.concatenate([pe[:, :, ROPE_HALF:], pe[:, :, :ROPE_HALF]], axis=-1)
    wqb = jnp.concatenate([wq[:, :, :HEAD_DIM].reshape(Q_LORA, WIDTH), pe.reshape(Q_LORA, HEADS * QK_ROPE),
                           pe_swapped.reshape(Q_LORA, HEADS * QK_ROPE)], axis=1).astype(BF16)
    wkv = w_kvb[0].reshape(KV_LORA, HEADS, 2 * HEAD_DIM)
    wkt = wkv[:, :, :HEAD_DIM].reshape(KV_LORA, WIDTH).T.astype(BF16)
    wv = wkv[:, :, HEAD_DIM:].reshape(KV_LORA, WIDTH).astype(BF16)
    inv_freq = ROPE_BASE ** (-jnp.arange(ROPE_HALF, dtype=F32) / ROPE_HALF)
    inv4 = jnp.tile(inv_freq, LANES // ROPE_HALF).reshape(1, LANES)
    sgn4 = jnp.tile(jnp.concatenate([-jnp.ones(ROPE_HALF, F32), jnp.ones(ROPE_HALF, F32)]), 2).reshape(1, LANES)
    wo_attn = w_out0[0, :WIDTH].astype(BF16)
    wo_gmlp = w_out0[0, WIDTH:].astype(BF16)
    cw = jnp.concatenate([conv_w[0], jnp.zeros((8 - CONV_W, D_MODEL), F32)], axis=0)

    q, kt, v = _mla_pre(xs, pos, _row(norm_mix[0]), win_mla, _row(q_norm[0]), _row(kv_norm[0]), wqb, wkt, wv,
                        inv4, sgn4)
    attn = _flash(q, kt, v)
    gm = _gmlp(xs, _row(norm_mix[0]), wu0, wv0, _row(v_ln_g[0]), _row(v_ln_b[0]), w_spatial[0],
               b_spatial[0].T)
    xs = _outproj(xs, attn, gm, wo_attn, wo_gmlp)
    xs = _ffn(xs, _row(norm_ffn[0]), w_gate[0].astype(BF16), w_up[0].astype(BF16), w_down[0].astype(BF16))
    xs = _ple(xs, p[0, 0], _row(norm_ple[0]), w_ple_gate[0].astype(BF16), w_ple_proj[0].astype(BF16),
              _row(norm_final), final_norm=False)

    xs = _conv_mixer(xs, _row(norm_mix[1]), w_in1[0].astype(BF16), cw, w_out1[0].astype(BF16))
    xs = _ffn(xs, _row(norm_ffn[1]), w_gate[1].astype(BF16), w_up[1].astype(BF16), w_down[1].astype(BF16))
    xs = _ple(xs, p[1, 0], _row(norm_ple[1]), w_ple_gate[1].astype(BF16), w_ple_proj[1].astype(BF16),
              _row(norm_final), final_norm=True)
    return xs.reshape(1, SEQ, D_MODEL)
```

```python
import functools
import math

import jax
import jax.numpy as jnp
from jax import lax
from jax.experimental import pallas as pl
from jax.experimental.pallas import tpu as pltpu

F32 = jnp.float32
BF16 = jnp.bfloat16

D_MODEL = 2048
SEQ = 16384
EPS = 1e-6
PLE_DIM = 256
HEADS = 8
HEAD_DIM = 128
CHUNK = 128
Q_LORA = 512
KV_LORA = 256
QK_ROPE = 64
ROPE_HALF = QK_ROPE // 2
QK_HEAD = HEAD_DIM + QK_ROPE
WIDTH = HEADS * HEAD_DIM
ROPE_BASE = 10000.0
D_FF = 5632
CONV_W = 3

VMEM_LIMIT_BYTES = 56 * 1024 * 1024
LANES = 128

ROW_TILE = 512
FFN_ROW_TILE = 1024
ATTN_TILE = 512
Q_TILE = 2048
HEAD_GROUP = 2
FF_TILE = 512
Q_SCALE = (QK_HEAD ** -0.5) * math.log2(math.e)
MASK_VALUE = -1e30


def _params(*semantics):
    return pltpu.CompilerParams(dimension_semantics=semantics, vmem_limit_bytes=VMEM_LIMIT_BYTES)


def _const_spec(shape):
    zeros = (0,) * len(shape)
    return pl.BlockSpec(shape, lambda *_: zeros, pipeline_mode=pl.Buffered(1))


def _rms(x, g):
    return x * lax.rsqrt(jnp.mean(x * x, axis=-1, keepdims=True) + EPS) * g


def _dot(a, b):
    return jnp.dot(a, b, preferred_element_type=F32)


def _dot_nt(a, b):
    return lax.dot_general(a, b, (((1,), (1,)), ((), ())), preferred_element_type=F32)


def _mla_pre_kernel(x_ref, pos_ref, g_ref, win_ref, qn_ref, kvn_ref, wqb_ref, wkt_ref, wv_ref,
                    inv_ref, sgn_ref, q_ref, kt_ref, v_ref):
    h = _rms(x_ref[...], g_ref[...]).astype(BF16)
    z = _dot(h, win_ref[...])
    q_lat = z[:, :Q_LORA]
    kv_lat = z[:, Q_LORA:Q_LORA + KV_LORA]
    kk = z[:, Q_LORA + KV_LORA:]

    ang = pos_ref[...].astype(F32) * inv_ref[...]
    cos = jnp.cos(ang)
    sin = jnp.sin(ang) * sgn_ref[...]

    qn = _rms(q_lat, qn_ref[...]).astype(BF16)
    qf = _dot(qn, wqb_ref[...])
    cos4 = jnp.concatenate([cos] * 4, axis=-1)
    sin4 = jnp.concatenate([sin] * 4, axis=-1)
    q_pe = qf[:, WIDTH:WIDTH + 512] * cos4 + qf[:, WIDTH + 512:] * sin4
    for hd in range(HEADS):
        q_ref[hd, :, 0:HEAD_DIM] = (qf[:, hd * HEAD_DIM:(hd + 1) * HEAD_DIM] * Q_SCALE).astype(BF16)
        q_ref[hd, :, HEAD_DIM:QK_HEAD] = (q_pe[:, hd * QK_ROPE:(hd + 1) * QK_ROPE] * Q_SCALE).astype(BF16)

    lane = lax.broadcasted_iota(jnp.int32, ang.shape, 1)
    t = kk * jnp.where(lane < QK_ROPE, cos, sin)
    k_pe = t + pltpu.roll(t, QK_ROPE, axis=1)
    k_pe_t = k_pe.T[:QK_ROPE, :].astype(BF16)

    kvn = _rms(kv_lat, kvn_ref[...]).astype(BF16)
    kn_t = _dot_nt(wkt_ref[...], kvn)
    vv = _dot(kvn, wv_ref[...])
    for hd in range(HEADS):
        kt_ref[hd, 0, 0:HEAD_DIM, :] = kn_t[hd * HEAD_DIM:(hd + 1) * HEAD_DIM, :].astype(BF16)
        kt_ref[hd, 0, HEAD_DIM:QK_HEAD, :] = k_pe_t
        v_ref[hd] = vv[:, hd * HEAD_DIM:(hd + 1) * HEAD_DIM].astype(BF16)


def _mla_pre(x, pos, g, win, qn, kvn, wqb, wkt, wv, inv4, sgn4):
    tm = ATTN_TILE
    n = SEQ // tm
    return pl.pallas_call(
        _mla_pre_kernel,
        grid=(n,),
        in_specs=[
            pl.BlockSpec((tm, D_MODEL), lambda i: (i, 0)),
            pl.BlockSpec((tm, 1), lambda i: (i, 0)),
            _const_spec(g.shape), _const_spec(win.shape), _const_spec(qn.shape), _const_spec(kvn.shape),
            _const_spec(wqb.shape), _const_spec(wkt.shape), _const_spec(wv.shape),
            _const_spec(inv4.shape), _const_spec(sgn4.shape),
        ],
        out_specs=[
            pl.BlockSpec((HEADS, tm, QK_HEAD), lambda i: (0, i, 0)),
            pl.BlockSpec((HEADS, 1, QK_HEAD, tm), lambda i: (0, i, 0, 0)),
            pl.BlockSpec((HEADS, tm, HEAD_DIM), lambda i: (0, i, 0)),
        ],
        out_shape=[
            jax.ShapeDtypeStruct((HEADS, SEQ, QK_HEAD), BF16),
            jax.ShapeDtypeStruct((HEADS, n, QK_HEAD, tm), BF16),
            jax.ShapeDtypeStruct((HEADS, SEQ, HEAD_DIM), BF16),
        ],
        compiler_params=_params("arbitrary"),
        name="mla_pre",
    )(x, pos, g, win, qn, kvn, wqb, wkt, wv, inv4, sgn4)


def _flash_kernel(q_ref, kt_ref, v_ref, o_ref, m_sc, l_sc, acc_sc):
    qi = pl.program_id(1)
    tq, tk = Q_TILE, ATTN_TILE
    ratio = tq // tk
    ncol = tk // LANES
    m_sc[...] = jnp.full(m_sc.shape, MASK_VALUE, F32)
    l_sc[...] = jnp.zeros(l_sc.shape, F32)
    acc_sc[...] = jnp.zeros(acc_sc.shape, F32)

    def step(j, diag):
        start = pl.multiple_of(j * tk, tk)
        rows = slice(None) if diag is None else slice(diag * tk, tq)
        for g in range(HEAD_GROUP):
            s = _dot(q_ref[g, rows, :], kt_ref[g, j])
            if diag is not None:
                row = lax.broadcasted_iota(jnp.int32, s.shape, 0)
                col = lax.broadcasted_iota(jnp.int32, s.shape, 1)
                s = jnp.where(col <= row, s, MASK_VALUE)
            cols = [s[:, c * LANES:(c + 1) * LANES] for c in range(ncol)]
            m_prev = m_sc[g, rows, :]
            m_cur = functools.reduce(jnp.maximum, cols)
            m_new = jnp.maximum(m_prev, jnp.max(m_cur, axis=-1, keepdims=True))
            alpha = jnp.exp2(m_prev - m_new)
            ps = [jnp.exp2(c - m_new) for c in cols]
            l_sc[g, rows, :] = alpha * l_sc[g, rows, :] + functools.reduce(jnp.add, ps)
            p = jnp.concatenate(ps, axis=-1).astype(BF16)
            acc_sc[g, rows, :] = alpha * acc_sc[g, rows, :] + _dot(p, v_ref[g, pl.ds(start, tk), :])
            m_sc[g, rows, :] = m_new

    def body(j, carry):
        step(j, None)
        return carry

    lax.fori_loop(0, qi * ratio, body, 0)
    for d in range(ratio):
        step(qi * ratio + d, d)
    for g in range(HEAD_GROUP):
        l = jnp.sum(l_sc[g], axis=-1, keepdims=True)
        o_ref[:, g * HEAD_DIM:(g + 1) * HEAD_DIM] = (acc_sc[g] / l).astype(BF16)


def _flash(q, kt, v):
    tq, tk = Q_TILE, ATTN_TILE
    hg = HEAD_GROUP
    return pl.pallas_call(
        _flash_kernel,
        grid=(HEADS // hg, SEQ // tq),
        in_specs=[
            pl.BlockSpec((hg, tq, QK_HEAD), lambda h, i: (h, i, 0)),
            pl.BlockSpec((hg, SEQ // tk, QK_HEAD, tk), lambda h, i: (h, 0, 0, 0), pipeline_mode=pl.Buffered(1)),
            pl.BlockSpec((hg, SEQ, HEAD_DIM), lambda h, i: (h, 0, 0), pipeline_mode=pl.Buffered(1)),
        ],
        out_specs=pl.BlockSpec((tq, hg * HEAD_DIM), lambda h, i: (i, h)),
        out_shape=jax.ShapeDtypeStruct((SEQ, WIDTH), BF16),
        scratch_shapes=[
            pltpu.VMEM((hg, tq, LANES), F32),
            pltpu.VMEM((hg, tq, LANES), F32),
            pltpu.VMEM((hg, tq, HEAD_DIM), F32),
        ],
        compiler_params=_params("arbitrary", "arbitrary"),
        name="mla_flash",
    )(q, kt, v)


def _gelu(x):
    return jax.nn.gelu(x)


def _gmlp_kernel(x_ref, g_ref, wu_ref, wv_ref, lng_ref, lnb_ref, ws_ref, bst_ref, o_ref):
    tm = x_ref.shape[0]
    h = _rms(x_ref[...], g_ref[...]).astype(BF16)
    u = _gelu(_dot(h, wu_ref[...]))
    gv = _gelu(_dot(h, wv_ref[...]))
    mu = jnp.mean(gv, axis=-1, keepdims=True)
    var = jnp.mean(jnp.square(gv - mu), axis=-1, keepdims=True)
    vn = ((gv - mu) * lax.rsqrt(var + EPS) * lng_ref[...] + lnb_ref[...]).astype(BF16)
    row = lax.broadcasted_iota(jnp.int32, (CHUNK, CHUNK), 0)
    col = lax.broadcasted_iota(jnp.int32, (CHUNK, CHUNK), 1)
    bst = bst_ref[...]
    for hd in range(HEADS):
        w = jnp.where(col <= row, ws_ref[hd], 0.0).astype(BF16)
        bias = bst[:, hd:hd + 1]
        cs = slice(hd * HEAD_DIM, (hd + 1) * HEAD_DIM)
        for c in range(tm // CHUNK):
            rs = slice(c * CHUNK, (c + 1) * CHUNK)
            gate = _dot(w, vn[rs, cs]) + bias
            o_ref[rs, cs] = (u[rs, cs] * gate).astype(BF16)


def _gmlp(x, g, wu, wv, lng, lnb, ws, bst):
    tm = ROW_TILE
    return pl.pallas_call(
        _gmlp_kernel,
        grid=(SEQ // tm,),
        in_specs=[
            pl.BlockSpec((tm, D_MODEL), lambda i: (i, 0)),
            _const_spec(g.shape), _const_spec(wu.shape), _const_spec(wv.shape),
            _const_spec(lng.shape), _const_spec(lnb.shape), _const_spec(ws.shape), _const_spec(bst.shape),
        ],
        out_specs=pl.BlockSpec((tm, WIDTH), lambda i: (i, 0)),
        out_shape=jax.ShapeDtypeStruct((SEQ, WIDTH), BF16),
        compiler_params=_params("arbitrary"),
        name="gmlp",
    )(x, g, wu, wv, lng, lnb, ws, bst)


def _outproj_kernel(x_ref, a_ref, b_ref, wa_ref, wb_ref, o_ref):
    o_ref[...] = x_ref[...] + _dot(a_ref[...], wa_ref[...]) + _dot(b_ref[...], wb_ref[...])


def _outproj(x, a, b, wa, wb):
    tm = ROW_TILE
    return pl.pallas_call(
        _outproj_kernel,
        grid=(SEQ // tm,),
        in_specs=[
            pl.BlockSpec((tm, D_MODEL), lambda i: (i, 0)),
            pl.BlockSpec((tm, WIDTH), lambda i: (i, 0)),
            pl.BlockSpec((tm, WIDTH), lambda i: (i, 0)),
            _const_spec(wa.shape), _const_spec(wb.shape),
        ],
        out_specs=pl.BlockSpec((tm, D_MODEL), lambda i: (i, 0)),
        out_shape=jax.ShapeDtypeStruct((SEQ, D_MODEL), F32),
        compiler_params=_params("arbitrary"),
        name="outproj0",
    )(x, a, b, wa, wb)


def _ffn_kernel(x_ref, g_ref, wg_ref, wu_ref, wd_ref, o_ref, h_sc):
    @pl.when(pl.program_id(1) == 0)
    def _():
        x = x_ref[...]
        h_sc[...] = _rms(x, g_ref[...]).astype(BF16)
        o_ref[...] = x

    h = h_sc[...]
    a = _dot(h, wg_ref[...])
    b = _dot(h, wu_ref[...])
    t = (a * jax.nn.sigmoid(a) * b).astype(BF16)
    o_ref[...] += _dot(t, wd_ref[...])


def _ffn(x, g, wg, wu, wd):
    tm, tf = FFN_ROW_TILE, FF_TILE
    return pl.pallas_call(
        _ffn_kernel,
        grid=(SEQ // tm, D_FF // tf),
        in_specs=[
            pl.BlockSpec((tm, D_MODEL), lambda i, j: (i, 0)),
            _const_spec(g.shape),
            pl.BlockSpec((D_MODEL, tf), lambda i, j: (0, j)),
            pl.BlockSpec((D_MODEL, tf), lambda i, j: (0, j)),
            pl.BlockSpec((tf, D_MODEL), lambda i, j: (j, 0)),
        ],
        out_specs=pl.BlockSpec((tm, D_MODEL), lambda i, j: (i, 0)),
        out_shape=jax.ShapeDtypeStruct((SEQ, D_MODEL), F32),
        scratch_shapes=[pltpu.VMEM((tm, D_MODEL), BF16)],
        compiler_params=_params("arbitrary", "arbitrary"),
        name="ffn",
    )(x, g, wg, wu, wd)


def _conv_kernel(x_ref, g_ref, wb_ref, wc_ref, wh_ref, cw_ref, wo_ref, o_ref, h_sc, tail_sc):
    i = pl.program_id(0)
    j = pl.program_id(1)

    @pl.when(j == 0)
    def _():
        x = x_ref[...]
        h_sc[...] = _rms(x, g_ref[...]).astype(BF16)
        o_ref[...] = x

    @pl.when(i == 0)
    def _():
        tail_sc[j] = jnp.zeros(tail_sc.shape[1:], F32)

    h = h_sc[...]
    gb = _dot(h, wb_ref[...])
    cz = _dot(h, wc_ref[...]) * _dot(h, wh_ref[...])
    tail = tail_sc[j]
    prev1 = tail[7:8, :]
    prev2 = tail[6:7, :]
    row = lax.broadcasted_iota(jnp.int32, cz.shape, 0)
    z1 = jnp.where(row == 0, prev1, pltpu.roll(cz, 1, axis=0))
    z2 = jnp.where(row == 0, prev2, jnp.where(row == 1, prev1, pltpu.roll(cz, 2, axis=0)))
    cw = cw_ref[...]
    y = cw[0:1, :] * z2 + cw[1:2, :] * z1 + cw[2:3, :] * cz
    tail_sc[j] = cz[cz.shape[0] - 8:, :]
    o_ref[...] += _dot((gb * y).astype(BF16), wo_ref[...])


def _conv_mixer(x, g, win, cw, wo):
    tm, tc = ROW_TILE, FF_TILE
    nc = D_MODEL // tc
    return pl.pallas_call(
        _conv_kernel,
        grid=(SEQ // tm, nc),
        in_specs=[
            pl.BlockSpec((tm, D_MODEL), lambda i, j: (i, 0)),
            _const_spec(g.shape),
            pl.BlockSpec((D_MODEL, tc), lambda i, j: (0, j)),
            pl.BlockSpec((D_MODEL, tc), lambda i, j: (0, j + nc)),
            pl.BlockSpec((D_MODEL, tc), lambda i, j: (0, j + 2 * nc)),
            pl.BlockSpec((8, tc), lambda i, j: (0, j)),
            pl.BlockSpec((tc, D_MODEL), lambda i, j: (j, 0)),
        ],
        out_specs=pl.BlockSpec((tm, D_MODEL), lambda i, j: (i, 0)),
        out_shape=jax.ShapeDtypeStruct((SEQ, D_MODEL), F32),
        scratch_shapes=[pltpu.VMEM((tm, D_MODEL), BF16), pltpu.VMEM((nc, 8, tc), F32)],
        compiler_params=_params("arbitrary", "arbitrary"),
        name="conv_mixer",
    )(x, g, win, win, win, cw, wo)


def _ple_kernel(x_ref, p_ref, g_ref, wg_ref, wp_ref, gf_ref, o_ref, *, final_norm):
    x = x_ref[...]
    h = _rms(x, g_ref[...]).astype(BF16)
    gate = jax.nn.sigmoid(_dot(h, wg_ref[...]))
    y = x + gate * _dot(p_ref[...].astype(BF16), wp_ref[...])
    if final_norm:
        y = _rms(y, gf_ref[...])
    o_ref[...] = y


def _ple(x, p, g, wg, wp, gf, final_norm):
    tm = ROW_TILE
    return pl.pallas_call(
        functools.partial(_ple_kernel, final_norm=final_norm),
        grid=(SEQ // tm,),
        in_specs=[
            pl.BlockSpec((tm, D_MODEL), lambda i: (i, 0)),
            pl.BlockSpec((tm, PLE_DIM), lambda i: (i, 0)),
            _const_spec(g.shape), _const_spec(wg.shape), _const_spec(wp.shape), _const_spec(gf.shape),
        ],
        out_specs=pl.BlockSpec((tm, D_MODEL), lambda i: (i, 0)),
        out_shape=jax.ShapeDtypeStruct((SEQ, D_MODEL), F32),
        compiler_params=_params("arbitrary"),
        name="ple",
    )(x, p, g, wg, wp, gf)


def _row(v):
    return v.reshape(1, -1)


def kernel(x, p, positions, norm_mix, norm_ffn, norm_ple, w_in0, q_norm, kv_norm, w_qb, w_kvb, v_ln_g,
           v_ln_b, w_spatial, b_spatial, w_out0, w_in1, conv_w, w_out1, w_gate, w_up, w_down, w_ple_gate,
           w_ple_proj, norm_final):
    assert x.shape == (1, SEQ, D_MODEL) and p.shape == (2, 1, SEQ, PLE_DIM)
    xs = x.reshape(SEQ, D_MODEL)
    pos = positions.reshape(SEQ, 1)

    o1, o2, o3 = Q_LORA, Q_LORA + KV_LORA, Q_LORA + KV_LORA + QK_ROPE
    w0 = w_in0[0]
    k1, k2 = w0[:, o2:o2 + ROPE_HALF], w0[:, o2 + ROPE_HALF:o3]
    win_mla = jnp.concatenate([w0[:, :o2], k1, k2, k2, k1], axis=1).astype(BF16)
    wu0 = w0[:, o3:o3 + WIDTH].astype(BF16)
    wv0 = w0[:, o3 + WIDTH:].astype(BF16)
    wq = w_qb[0].reshape(Q_LORA, HEADS, QK_HEAD)
    pe = wq[:, :, HEAD_DIM:]
    pe_swapped = jnp.concatenate([pe[:, :, ROPE_HALF:], pe[:, :, :ROPE_HALF]], axis=-1)
    wqb = jnp.concatenate([wq[:, :, :HEAD_DIM].reshape(Q_LORA, WIDTH), pe.reshape(Q_LORA, HEADS * QK_ROPE),
                           pe_swapped.reshape(Q_LORA, HEADS * QK_ROPE)], axis=1).astype(BF16)
    wkv = w_kvb[0].reshape(KV_LORA, HEADS, 2 * HEAD_DIM)
    wkt = wkv[:, :, :HEAD_DIM].reshape(KV_LORA, WIDTH).T.astype(BF16)
    wv = wkv[:, :, HEAD_DIM:].reshape(KV_LORA, WIDTH).astype(BF16)
    inv_freq = ROPE_BASE ** (-jnp.arange(ROPE_HALF, dtype=F32) / ROPE_HALF)
    inv4 = jnp.tile(inv_freq, LANES // ROPE_HALF).reshape(1, LANES)
    sgn4 = jnp.tile(jnp.concatenate([-jnp.ones(ROPE_HALF, F32), jnp.ones(ROPE_HALF, F32)]), 2).reshape(1, LANES)
    wo_attn = w_out0[0, :WIDTH].astype(BF16)
    wo_gmlp = w_out0[0, WIDTH:].astype(BF16)
    cw = jnp.concatenate([conv_w[0], jnp.zeros((8 - CONV_W, D_MODEL), F32)], axis=0)

    q, kt, v = _mla_pre(xs, pos, _row(norm_mix[0]), win_mla, _row(q_norm[0]), _row(kv_norm[0]), wqb, wkt, wv,
                        inv4, sgn4)
    attn = _flash(q, kt, v)
    gm = _gmlp(xs, _row(norm_mix[0]), wu0, wv0, _row(v_ln_g[0]), _row(v_ln_b[0]), w_spatial[0],
               b_spatial[0].T)
    xs = _outproj(xs, attn, gm, wo_attn, wo_gmlp)
    xs = _ffn(xs, _row(norm_ffn[0]), w_gate[0].astype(BF16), w_up[0].astype(BF16), w_down[0].astype(BF16))
    xs = _ple(xs, p[0, 0], _row(norm_ple[0]), w_ple_gate[0].astype(BF16), w_ple_proj[0].astype(BF16),
              _row(norm_final), final_norm=False)

    xs = _conv_mixer(xs, _row(norm_mix[1]), w_in1[0].astype(BF16), cw, w_out1[0].astype(BF16))
    xs = _ffn(xs, _row(norm_ffn[1]), w_gate[1].astype(BF16), w_up[1].astype(BF16), w_down[1].astype(BF16))
    xs = _ple(xs, p[1, 0], _row(norm_ple[1]), w_ple_gate[1].astype(BF16), w_ple_proj[1].astype(BF16),
              _row(norm_final), final_norm=True)
    return xs.reshape(1, SEQ, D_MODEL)
```

```python
import functools
import math

import jax
import jax.numpy as jnp
from jax import lax
from jax.experimental import pallas as pl
from jax.experimental.pallas import tpu as pltpu

F32 = jnp.float32
BF16 = jnp.bfloat16

D_MODEL = 2048
SEQ = 16384
EPS = 1e-6
PLE_DIM = 256
HEADS = 8
HEAD_DIM = 128
CHUNK = 128
Q_LORA = 512
KV_LORA = 256
QK_ROPE = 64
ROPE_HALF = QK_ROPE // 2
QK_HEAD = HEAD_DIM + QK_ROPE
WIDTH = HEADS * HEAD_DIM
ROPE_BASE = 10000.0
D_FF = 5632
CONV_W = 3

VMEM_LIMIT_BYTES = 56 * 1024 * 1024
LANES = 128

ROW_TILE = 512
FFN_ROW_TILE = 1024
ATTN_TILE = 512
Q_TILE = 2048
HEAD_GROUP = 2
FF_TILE = 512
Q_SCALE = (QK_HEAD ** -0.5) * math.log2(math.e)
MASK_VALUE = -1e30


def _params(*semantics):
    return pltpu.CompilerParams(dimension_semantics=semantics, vmem_limit_bytes=VMEM_LIMIT_BYTES)


def _const_spec(shape):
    zeros = (0,) * len(shape)
    return pl.BlockSpec(shape, lambda *_: zeros, pipeline_mode=pl.Buffered(1))


def _rms(x, g):
    return x * lax.rsqrt(jnp.mean(x * x, axis=-1, keepdims=True) + EPS) * g


def _dot(a, b):
    return jnp.dot(a, b, preferred_element_type=F32)


def _dot_nt(a, b):
    return lax.dot_general(a, b, (((1,), (1,)), ((), ())), preferred_element_type=F32)


def _mla_pre_kernel(x_ref, pos_ref, g_ref, win_ref, qn_ref, kvn_ref, wqb_ref, wkt_ref, wv_ref,
                    inv_ref, sgn_ref, q_ref, kt_ref, v_ref):
    h = _rms(x_ref[...], g_ref[...]).astype(BF16)
    z = _dot(h, win_ref[...])
    q_lat = z[:, :Q_LORA]
    kv_lat = z[:, Q_LORA:Q_LORA + KV_LORA]
    kk = z[:, Q_LORA + KV_LORA:]

    ang = pos_ref[...].astype(F32) * inv_ref[...]
    cos = jnp.cos(ang)
    sin = jnp.sin(ang) * sgn_ref[...]

    qn = _rms(q_lat, qn_ref[...]).astype(BF16)
    qf = _dot(qn, wqb_ref[...])
    cos4 = jnp.concatenate([cos] * 4, axis=-1)
    sin4 = jnp.concatenate([sin] * 4, axis=-1)
    q_pe = qf[:, WIDTH:WIDTH + 512] * cos4 + qf[:, WIDTH + 512:] * sin4
    for hd in range(HEADS):
        q_ref[hd, :, 0:HEAD_DIM] = (qf[:, hd * HEAD_DIM:(hd + 1) * HEAD_DIM] * Q_SCALE).astype(BF16)
        q_ref[hd, :, HEAD_DIM:QK_HEAD] = (q_pe[:, hd * QK_ROPE:(hd + 1) * QK_ROPE] * Q_SCALE).astype(BF16)

    lane = lax.broadcasted_iota(jnp.int32, ang.shape, 1)
    t = kk * jnp.where(lane < QK_ROPE, cos, sin)
    k_pe = t + pltpu.roll(t, QK_ROPE, axis=1)
    k_pe_t = k_pe.T[:QK_ROPE, :].astype(BF16)

    kvn = _rms(kv_lat, kvn_ref[...]).astype(BF16)
    kn_t = _dot_nt(wkt_ref[...], kvn)
    vv = _dot(kvn, wv_ref[...])
    for hd in range(HEADS):
        kt_ref[hd, 0, 0:HEAD_DIM, :] = kn_t[hd * HEAD_DIM:(hd + 1) * HEAD_DIM, :].astype(BF16)
        kt_ref[hd, 0, HEAD_DIM:QK_HEAD, :] = k_pe_t
        v_ref[hd] = vv[:, hd * HEAD_DIM:(hd + 1) * HEAD_DIM].astype(BF16)


def _mla_pre(x, pos, g, win, qn, kvn, wqb, wkt, wv, inv4, sgn4):
    tm = ATTN_TILE
    n = SEQ // tm
    return pl.pallas_call(
        _mla_pre_kernel,
        grid=(n,),
        in_specs=[
            pl.BlockSpec((tm, D_MODEL), lambda i: (i, 0)),
            pl.BlockSpec((tm, 1), lambda i: (i, 0)),
            _const_spec(g.shape), _const_spec(win.shape), _const_spec(qn.shape), _const_spec(kvn.shape),
            _const_spec(wqb.shape), _const_spec(wkt.shape), _const_spec(wv.shape),
            _const_spec(inv4.shape), _const_spec(sgn4.shape),
        ],
        out_specs=[
            pl.BlockSpec((HEADS, tm, QK_HEAD), lambda i: (0, i, 0)),
            pl.BlockSpec((HEADS, 1, QK_HEAD, tm), lambda i: (0, i, 0, 0)),
            pl.BlockSpec((HEADS, tm, HEAD_DIM), lambda i: (0, i, 0)),
        ],
        out_shape=[
            jax.ShapeDtypeStruct((HEADS, SEQ, QK_HEAD), BF16),
            jax.ShapeDtypeStruct((HEADS, n, QK_HEAD, tm), BF16),
            jax.ShapeDtypeStruct((HEADS, SEQ, HEAD_DIM), BF16),
        ],
        compiler_params=_params("arbitrary"),
        name="mla_pre",
    )(x, pos, g, win, qn, kvn, wqb, wkt, wv, inv4, sgn4)


def _flash_kernel(q_ref, kt_ref, v_ref, o_ref, m_sc, l_sc, acc_sc):
    qi = pl.program_id(1)
    tq, tk = Q_TILE, ATTN_TILE
    ratio = tq // tk
    ncol = tk // LANES
    m_sc[...] = jnp.full(m_sc.shape, MASK_VALUE, F32)
    l_sc[...] = jnp.zeros(l_sc.shape, F32)
    acc_sc[...] = jnp.zeros(acc_sc.shape, F32)

    def step(j, diag):
        start = pl.multiple_of(j * tk, tk)
        rows = slice(None) if diag is None else slice(diag * tk, tq)
        for g in range(HEAD_GROUP):
            s = _dot(q_ref[g, rows, :], kt_ref[g, j])
            if diag is not None:
                row = lax.broadcasted_iota(jnp.int32, s.shape, 0)
                col = lax.broadcasted_iota(jnp.int32, s.shape, 1)
                s = jnp.where(col <= row, s, MASK_VALUE)
            cols = [s[:, c * LANES:(c + 1) * LANES] for c in range(ncol)]
            m_prev = m_sc[g, rows, :]
            m_cur = functools.reduce(jnp.maximum, cols)
            m_new = jnp.maximum(m_prev, jnp.max(m_cur, axis=-1, keepdims=True))
            alpha = jnp.exp2(m_prev - m_new)
            ps = [jnp.exp2(c - m_new) for c in cols]
            l_sc[g, rows, :] = alpha * l_sc[g, rows, :] + functools.reduce(jnp.add, ps)
            p = jnp.concatenate(ps, axis=-1).astype(BF16)
            acc_sc[g, rows, :] = alpha * acc_sc[g, rows, :] + _dot(p, v_ref[g, pl.ds(start, tk), :])
            m_sc[g, rows, :] = m_new

    def body(j, carry):
        step(j, None)
        return carry

    lax.fori_loop(0, qi * ratio, body, 0)
    for d in range(ratio):
        step(qi * ratio + d, d)
    for g in range(HEAD_GROUP):
        l = jnp.sum(l_sc[g], axis=-1, keepdims=True)
        o_ref[:, g * HEAD_DIM:(g + 1) * HEAD_DIM] = (acc_sc[g] / l).astype(BF16)


def _flash(q, kt, v):
    tq, tk = Q_TILE, ATTN_TILE
    hg = HEAD_GROUP
    return pl.pallas_call(
        _flash_kernel,
        grid=(HEADS // hg, SEQ // tq),
        in_specs=[
            pl.BlockSpec((hg, tq, QK_HEAD), lambda h, i: (h, i, 0)),
            pl.BlockSpec((hg, SEQ // tk, QK_HEAD, tk), lambda h, i: (h, 0, 0, 0), pipeline_mode=pl.Buffered(1)),
            pl.BlockSpec((hg, SEQ, HEAD_DIM), lambda h, i: (h, 0, 0), pipeline_mode=pl.Buffered(1)),
        ],
        out_specs=pl.BlockSpec((tq, hg * HEAD_DIM), lambda h, i: (i, h)),
        out_shape=jax.ShapeDtypeStruct((SEQ, WIDTH), BF16),
        scratch_shapes=[
            pltpu.VMEM((hg, tq, LANES), F32),
            pltpu.VMEM((hg, tq, LANES), F32),
            pltpu.VMEM((hg, tq, HEAD_DIM), F32),
        ],
        compiler_params=_params("arbitrary", "arbitrary"),
        name="mla_flash",
    )(q, kt, v)


def _gelu(x):
    return jax.nn.gelu(x)


def _gmlp_kernel(x_ref, g_ref, wu_ref, wv_ref, lng_ref, lnb_ref, ws_ref, bst_ref, o_ref):
    tm = x_ref.shape[0]
    h = _rms(x_ref[...], g_ref[...]).astype(BF16)
    u = _gelu(_dot(h, wu_ref[...]))
    gv = _gelu(_dot(h, wv_ref[...]))
    mu = jnp.mean(gv, axis=-1, keepdims=True)
    var = jnp.mean(jnp.square(gv - mu), axis=-1, keepdims=True)
    vn = ((gv - mu) * lax.rsqrt(var + EPS) * lng_ref[...] + lnb_ref[...]).astype(BF16)
    row = lax.broadcasted_iota(jnp.int32, (CHUNK, CHUNK), 0)
    col = lax.broadcasted_iota(jnp.int32, (CHUNK, CHUNK), 1)
    bst = bst_ref[...]
    for hd in range(HEADS):
        w = jnp.where(col <= row, ws_ref[hd], 0.0).astype(BF16)
        bias = bst[:, hd:hd + 1]
        cs = slice(hd * HEAD_DIM, (hd + 1) * HEAD_DIM)
        for c in range(tm // CHUNK):
            rs = slice(c * CHUNK, (c + 1) * CHUNK)
            gate = _dot(w, vn[rs, cs]) + bias
            o_ref[rs, cs] = (u[rs, cs] * gate).astype(BF16)


def _gmlp(x, g, wu, wv, lng, lnb, ws, bst):
    tm = ROW_TILE
    return pl.pallas_call(
        _gmlp_kernel,
        grid=(SEQ // tm,),
        in_specs=[
            pl.BlockSpec((tm, D_MODEL), lambda i: (i, 0)),
            _const_spec(g.shape), _const_spec(wu.shape), _const_spec(wv.shape),
            _const_spec(lng.shape), _const_spec(lnb.shape), _const_spec(ws.shape), _const_spec(bst.shape),
        ],
        out_specs=pl.BlockSpec((tm, WIDTH), lambda i: (i, 0)),
        out_shape=jax.ShapeDtypeStruct((SEQ, WIDTH), BF16),
        compiler_params=_params("arbitrary"),
        name="gmlp",
    )(x, g, wu, wv, lng, lnb, ws, bst)


def _outproj_kernel(x_ref, a_ref, b_ref, wa_ref, wb_ref, o_ref):
    o_ref[...] = x_ref[...] + _dot(a_ref[...], wa_ref[...]) + _dot(b_ref[...], wb_ref[...])


def _outproj(x, a, b, w):
    tm = ROW_TILE
    return pl.pallas_call(
        _outproj_kernel,
        grid=(SEQ // tm,),
        in_specs=[
            pl.BlockSpec((tm, D_MODEL), lambda i: (i, 0)),
            pl.BlockSpec((tm, WIDTH), lambda i: (i, 0)),
            pl.BlockSpec((tm, WIDTH), lambda i: (i, 0)),
            pl.BlockSpec((WIDTH, D_MODEL), lambda i: (0, 0), pipeline_mode=pl.Buffered(1)),
            pl.BlockSpec((WIDTH, D_MODEL), lambda i: (1, 0), pipeline_mode=pl.Buffered(1)),
        ],
        out_specs=pl.BlockSpec((tm, D_MODEL), lambda i: (i, 0)),
        out_shape=jax.ShapeDtypeStruct((SEQ, D_MODEL), F32),
        compiler_params=_params("arbitrary"),
        name="outproj0",
    )(x, a, b, w, w)


def _ffn_kernel(x_ref, g_ref, wg_ref, wu_ref, wd_ref, o_ref, h_sc):
    @pl.when(pl.program_id(1) == 0)
    def _():
        x = x_ref[...]
        h_sc[...] = _rms(x, g_ref[...]).astype(BF16)
        o_ref[...] = x

    h = h_sc[...]
    a = _dot(h, wg_ref[...])
    b = _dot(h, wu_ref[...])
    t = (a * jax.nn.sigmoid(a) * b).astype(BF16)
    o_ref[...] += _dot(t, wd_ref[...])


def _ffn(x, g, wg, wu, wd, layer):
    tm, tf = FFN_ROW_TILE, FF_TILE
    return pl.pallas_call(
        _ffn_kernel,
        grid=(SEQ // tm, D_FF // tf),
        in_specs=[
            pl.BlockSpec((tm, D_MODEL), lambda i, j: (i, 0)),
            _const_spec(g.shape),
            pl.BlockSpec((None, D_MODEL, tf), lambda i, j: (layer, 0, j)),
            pl.BlockSpec((None, D_MODEL, tf), lambda i, j: (layer, 0, j)),
            pl.BlockSpec((None, tf, D_MODEL), lambda i, j: (layer, j, 0)),
        ],
        out_specs=pl.BlockSpec((tm, D_MODEL), lambda i, j: (i, 0)),
        out_shape=jax.ShapeDtypeStruct((SEQ, D_MODEL), F32),
        scratch_shapes=[pltpu.VMEM((tm, D_MODEL), BF16)],
        compiler_params=_params("arbitrary", "arbitrary"),
        name="ffn",
    )(x, g, wg, wu, wd)


def _conv_kernel(x_ref, g_ref, wb_ref, wc_ref, wh_ref, cw_ref, wo_ref, o_ref, h_sc, tail_sc):
    i = pl.program_id(0)
    j = pl.program_id(1)

    @pl.when(j == 0)
    def _():
        x = x_ref[...]
        h_sc[...] = _rms(x, g_ref[...]).astype(BF16)
        o_ref[...] = x

    @pl.when(i == 0)
    def _():
        tail_sc[j] = jnp.zeros(tail_sc.shape[1:], F32)

    h = h_sc[...]
    tail = tail_sc[j]
    cw = cw_ref[...]
    half = wb_ref.shape[1] // 2
    acc = None
    for s in range(2):
        cs = slice(s * half, (s + 1) * half)
        gb = _dot(h, wb_ref[:, cs])
        cz = _dot(h, wc_ref[:, cs]) * _dot(h, wh_ref[:, cs])
        prev1 = tail[7:8, cs]
        prev2 = tail[6:7, cs]
        row = lax.broadcasted_iota(jnp.int32, cz.shape, 0)
        z1 = jnp.where(row == 0, prev1, pltpu.roll(cz, 1, axis=0))
        z2 = jnp.where(row == 0, prev2, jnp.where(row == 1, prev1, pltpu.roll(cz, 2, axis=0)))
        y = cw[0:1, cs] * z2 + cw[1:2, cs] * z1 + cw[2:3, cs] * cz
        tail_sc[j, :, cs] = cz[cz.shape[0] - 8:, :]
        part = _dot((gb * y).astype(BF16), wo_ref[cs, :])
        acc = part if acc is None else acc + part
    o_ref[...] += acc


def _conv_mixer(x, g, win, cw, wo):
    tm, tc = ROW_TILE, FF_TILE
    nc = D_MODEL // tc
    return pl.pallas_call(
        _conv_kernel,
        grid=(SEQ // tm, nc),
        in_specs=[
            pl.BlockSpec((tm, D_MODEL), lambda i, j: (i, 0)),
            _const_spec(g.shape),
            pl.BlockSpec((D_MODEL, tc), lambda i, j: (0, j)),
            pl.BlockSpec((D_MODEL, tc), lambda i, j: (0, j + nc)),
            pl.BlockSpec((D_MODEL, tc), lambda i, j: (0, j + 2 * nc)),
            pl.BlockSpec((8, tc), lambda i, j: (0, j)),
            pl.BlockSpec((tc, D_MODEL), lambda i, j: (j, 0)),
        ],
        out_specs=pl.BlockSpec((tm, D_MODEL), lambda i, j: (i, 0)),
        out_shape=jax.ShapeDtypeStruct((SEQ, D_MODEL), F32),
        scratch_shapes=[pltpu.VMEM((tm, D_MODEL), BF16), pltpu.VMEM((nc, 8, tc), F32)],
        compiler_params=_params("arbitrary", "arbitrary"),
        name="conv_mixer",
    )(x, g, win, win, win, cw, wo)


def _ple_kernel(x_ref, p_ref, g_ref, wg_ref, wp_ref, gf_ref, o_ref, *, final_norm):
    x = x_ref[...]
    h = _rms(x, g_ref[...]).astype(BF16)
    gate = jax.nn.sigmoid(_dot(h, wg_ref[...]))
    y = x + gate * _dot(p_ref[...].astype(BF16), wp_ref[...])
    if final_norm:
        y = _rms(y, gf_ref[...])
    o_ref[...] = y


def _ple(x, p, g, wg, wp, gf, layer, final_norm):
    tm = ROW_TILE
    return pl.pallas_call(
        functools.partial(_ple_kernel, final_norm=final_norm),
        grid=(SEQ // tm,),
        in_specs=[
            pl.BlockSpec((tm, D_MODEL), lambda i: (i, 0)),
            pl.BlockSpec((None, None, tm, PLE_DIM), lambda i: (layer, 0, i, 0)),
            _const_spec(g.shape),
            pl.BlockSpec((None, D_MODEL, D_MODEL), lambda i: (layer, 0, 0), pipeline_mode=pl.Buffered(1)),
            pl.BlockSpec((None, PLE_DIM, D_MODEL), lambda i: (layer, 0, 0), pipeline_mode=pl.Buffered(1)),
            _const_spec(gf.shape),
        ],
        out_specs=pl.BlockSpec((tm, D_MODEL), lambda i: (i, 0)),
        out_shape=jax.ShapeDtypeStruct((SEQ, D_MODEL), F32),
        compiler_params=_params("arbitrary"),
        name="ple",
    )(x, p, g, wg, wp, gf)


def _row(v):
    return v.reshape(1, -1)


def kernel(x, p, positions, norm_mix, norm_ffn, norm_ple, w_in0, q_norm, kv_norm, w_qb, w_kvb, v_ln_g,
           v_ln_b, w_spatial, b_spatial, w_out0, w_in1, conv_w, w_out1, w_gate, w_up, w_down, w_ple_gate,
           w_ple_proj, norm_final):
    assert x.shape == (1, SEQ, D_MODEL) and p.shape == (2, 1, SEQ, PLE_DIM)
    xs = x.reshape(SEQ, D_MODEL)
    pos = positions.reshape(SEQ, 1)

    o1, o2, o3 = Q_LORA, Q_LORA + KV_LORA, Q_LORA + KV_LORA + QK_ROPE
    w0 = w_in0[0]
    k1, k2 = w0[:, o2:o2 + ROPE_HALF], w0[:, o2 + ROPE_HALF:o3]
    win_mla = jnp.concatenate([w0[:, :o2], k1, k2, k2, k1], axis=1).astype(BF16)
    wu0 = w0[:, o3:o3 + WIDTH].astype(BF16)
    wv0 = w0[:, o3 + WIDTH:].astype(BF16)
    wq = w_qb[0].reshape(Q_LORA, HEADS, QK_HEAD)
    pe = wq[:, :, HEAD_DIM:]
    pe_swapped = jnp.concatenate([pe[:, :, ROPE_HALF:], pe[:, :, :ROPE_HALF]], axis=-1)
    wqb = jnp.concatenate([wq[:, :, :HEAD_DIM].reshape(Q_LORA, WIDTH), pe.reshape(Q_LORA, HEADS * QK_ROPE),
                           pe_swapped.reshape(Q_LORA, HEADS * QK_ROPE)], axis=1).astype(BF16)
    wkv = w_kvb[0].reshape(KV_LORA, HEADS, 2 * HEAD_DIM)
    wkt = wkv[:, :, :HEAD_DIM].reshape(KV_LORA, WIDTH).T.astype(BF16)
    wv = wkv[:, :, HEAD_DIM:].reshape(KV_LORA, WIDTH).astype(BF16)
    inv_freq = ROPE_BASE ** (-jnp.arange(ROPE_HALF, dtype=F32) / ROPE_HALF)
    inv4 = jnp.tile(inv_freq, LANES // ROPE_HALF).reshape(1, LANES)
    sgn4 = jnp.tile(jnp.concatenate([-jnp.ones(ROPE_HALF, F32), jnp.ones(ROPE_HALF, F32)]), 2).reshape(1, LANES)
    cw = jnp.concatenate([conv_w[0], jnp.zeros((8 - CONV_W, D_MODEL), F32)], axis=0)
    wg_ffn, wu_ffn, wd_ffn = w_gate.astype(BF16), w_up.astype(BF16), w_down.astype(BF16)
    wg_ple, wp_ple = w_ple_gate.astype(BF16), w_ple_proj.astype(BF16)

    q, kt, v = _mla_pre(xs, pos, _row(norm_mix[0]), win_mla, _row(q_norm[0]), _row(kv_norm[0]), wqb, wkt, wv,
                        inv4, sgn4)
    attn = _flash(q, kt, v)
    gm = _gmlp(xs, _row(norm_mix[0]), wu0, wv0, _row(v_ln_g[0]), _row(v_ln_b[0]), w_spatial[0],
               b_spatial[0].T)
    xs = _outproj(xs, attn, gm, w_out0[0].astype(BF16))
    xs = _ffn(xs, _row(norm_ffn[0]), wg_ffn, wu_ffn, wd_ffn, layer=0)
    xs = _ple(xs, p, _row(norm_ple[0]), wg_ple, wp_ple, _row(norm_final), layer=0, final_norm=False)

    xs = _conv_mixer(xs, _row(norm_mix[1]), w_in1[0].astype(BF16), cw, w_out1[0].astype(BF16))
    xs = _ffn(xs, _row(norm_ffn[1]), wg_ffn, wu_ffn, wd_ffn, layer=1)
    xs = _ple(xs, p, _row(norm_ple[1]), wg_ple, wp_ple, _row(norm_final), layer=1, final_norm=True)
    return xs.reshape(1, SEQ, D_MODEL)
```

```python
import functools
import math

import jax
import jax.numpy as jnp
from jax import lax
from jax.experimental import pallas as pl
from jax.experimental.pallas import tpu as pltpu

F32 = jnp.float32
BF16 = jnp.bfloat16

D_MODEL = 2048
SEQ = 16384
EPS = 1e-6
PLE_DIM = 256
HEADS = 8
HEAD_DIM = 128
CHUNK = 128
Q_LORA = 512
KV_LORA = 256
QK_ROPE = 64
ROPE_HALF = QK_ROPE // 2
QK_HEAD = HEAD_DIM + QK_ROPE
WIDTH = HEADS * HEAD_DIM
ROPE_BASE = 10000.0
D_FF = 5632
CONV_W = 3

VMEM_LIMIT_BYTES = 56 * 1024 * 1024
LANES = 128

ROW_TILE = 512
FFN_ROW_TILE = 1024
ATTN_TILE = 512
Q_TILE = 2048
HEAD_GROUP = 2
FF_TILE = 512
Q_SCALE = (QK_HEAD ** -0.5) * math.log2(math.e)
MASK_VALUE = -1e30


def _params(*semantics):
    return pltpu.CompilerParams(dimension_semantics=semantics, vmem_limit_bytes=VMEM_LIMIT_BYTES)


def _const_spec(shape):
    zeros = (0,) * len(shape)
    return pl.BlockSpec(shape, lambda *_: zeros, pipeline_mode=pl.Buffered(1))


def _rms(x, g):
    return x * lax.rsqrt(jnp.mean(x * x, axis=-1, keepdims=True) + EPS) * g


def _dot(a, b):
    return jnp.dot(a, b, preferred_element_type=F32)


def _dot_nt(a, b):
    return lax.dot_general(a, b, (((1,), (1,)), ((), ())), preferred_element_type=F32)


def _mla_pre_kernel(x_ref, pos_ref, g_ref, win_ref, qn_ref, kvn_ref, wqb_ref, wkt_ref, wv_ref,
                    inv_ref, sgn_ref, q_ref, kt_ref, v_ref):
    h = _rms(x_ref[...], g_ref[...]).astype(BF16)
    z = _dot(h, win_ref[...])
    q_lat = z[:, :Q_LORA]
    kv_lat = z[:, Q_LORA:Q_LORA + KV_LORA]
    kk = z[:, Q_LORA + KV_LORA:]

    ang = pos_ref[...].astype(F32) * inv_ref[...]
    cos = jnp.cos(ang)
    sin = jnp.sin(ang) * sgn_ref[...]

    qn = _rms(q_lat, qn_ref[...]).astype(BF16)
    qf = _dot(qn, wqb_ref[...])
    cos4 = jnp.concatenate([cos] * 4, axis=-1)
    sin4 = jnp.concatenate([sin] * 4, axis=-1)
    q_pe = qf[:, WIDTH:WIDTH + 512] * cos4 + qf[:, WIDTH + 512:] * sin4
    for hd in range(HEADS):
        q_ref[hd, :, 0:HEAD_DIM] = (qf[:, hd * HEAD_DIM:(hd + 1) * HEAD_DIM] * Q_SCALE).astype(BF16)
        q_ref[hd, :, HEAD_DIM:QK_HEAD] = (q_pe[:, hd * QK_ROPE:(hd + 1) * QK_ROPE] * Q_SCALE).astype(BF16)

    lane = lax.broadcasted_iota(jnp.int32, ang.shape, 1)
    t = kk * jnp.where(lane < QK_ROPE, cos, sin)
    k_pe = t + pltpu.roll(t, QK_ROPE, axis=1)
    k_pe_t = k_pe.T[:QK_ROPE, :].astype(BF16)

    kvn = _rms(kv_lat, kvn_ref[...]).astype(BF16)
    kn_t = _dot_nt(wkt_ref[...], kvn)
    vv = _dot(kvn, wv_ref[...])
    for hd in range(HEADS):
        kt_ref[hd, 0, 0:HEAD_DIM, :] = kn_t[hd * HEAD_DIM:(hd + 1) * HEAD_DIM, :].astype(BF16)
        kt_ref[hd, 0, HEAD_DIM:QK_HEAD, :] = k_pe_t
        v_ref[hd] = vv[:, hd * HEAD_DIM:(hd + 1) * HEAD_DIM].astype(BF16)


def _mla_pre(x, pos, g, win, qn, kvn, wqb, wkt, wv, inv4, sgn4):
    tm = ATTN_TILE
    n = SEQ // tm
    return pl.pallas_call(
        _mla_pre_kernel,
        grid=(n,),
        in_specs=[
            pl.BlockSpec((tm, D_MODEL), lambda i: (i, 0)),
            pl.BlockSpec((tm, 1), lambda i: (i, 0)),
            _const_spec(g.shape), _const_spec(win.shape), _const_spec(qn.shape), _const_spec(kvn.shape),
            _const_spec(wqb.shape), _const_spec(wkt.shape), _const_spec(wv.shape),
            _const_spec(inv4.shape), _const_spec(sgn4.shape),
        ],
        out_specs=[
            pl.BlockSpec((HEADS, tm, QK_HEAD), lambda i: (0, i, 0)),
            pl.BlockSpec((HEADS, 1, QK_HEAD, tm), lambda i: (0, i, 0, 0)),
            pl.BlockSpec((HEADS, tm, HEAD_DIM), lambda i: (0, i, 0)),
        ],
        out_shape=[
            jax.ShapeDtypeStruct((HEADS, SEQ, QK_HEAD), BF16),
            jax.ShapeDtypeStruct((HEADS, n, QK_HEAD, tm), BF16),
            jax.ShapeDtypeStruct((HEADS, SEQ, HEAD_DIM), BF16),
        ],
        compiler_params=_params("arbitrary"),
        name="mla_pre",
    )(x, pos, g, win, qn, kvn, wqb, wkt, wv, inv4, sgn4)


def _flash_kernel(q_ref, kt_ref, v_ref, o_ref, m_sc, l_sc, acc_sc):
    qi = pl.program_id(1)
    tq, tk = Q_TILE, ATTN_TILE
    ratio = tq // tk
    ncol = tk // LANES
    m_sc[...] = jnp.full(m_sc.shape, MASK_VALUE, F32)
    l_sc[...] = jnp.zeros(l_sc.shape, F32)
    acc_sc[...] = jnp.zeros(acc_sc.shape, F32)

    def step(j, diag):
        start = pl.multiple_of(j * tk, tk)
        rows = slice(None) if diag is None else slice(diag * tk, tq)
        for g in range(HEAD_GROUP):
            s = _dot(q_ref[g, rows, :], kt_ref[g, j])
            if diag is not None:
                row = lax.broadcasted_iota(jnp.int32, s.shape, 0)
                col = lax.broadcasted_iota(jnp.int32, s.shape, 1)
                s = jnp.where(col <= row, s, MASK_VALUE)
            cols = [s[:, c * LANES:(c + 1) * LANES] for c in range(ncol)]
            m_prev = m_sc[g, rows, :]
            m_cur = functools.reduce(jnp.maximum, cols)
            m_new = jnp.maximum(m_prev, jnp.max(m_cur, axis=-1, keepdims=True))
            alpha = jnp.exp2(m_prev - m_new)
            ps = [jnp.exp2((c - m_new).astype(BF16)) for c in cols]
            l_sc[g, rows, :] = alpha * l_sc[g, rows, :] + functools.reduce(jnp.add, ps).astype(F32)
            p = jnp.concatenate(ps, axis=-1)
            acc_sc[g, rows, :] = alpha * acc_sc[g, rows, :] + _dot(p, v_ref[g, pl.ds(start, tk), :])
            m_sc[g, rows, :] = m_new

    def body(j, carry):
        step(j, None)
        return carry

    lax.fori_loop(0, qi * ratio, body, 0)
    for d in range(ratio):
        step(qi * ratio + d, d)
    for g in range(HEAD_GROUP):
        l = jnp.sum(l_sc[g], axis=-1, keepdims=True)
        o_ref[:, g * HEAD_DIM:(g + 1) * HEAD_DIM] = (acc_sc[g] / l).astype(BF16)


def _flash(q, kt, v):
    tq, tk = Q_TILE, ATTN_TILE
    hg = HEAD_GROUP
    return pl.pallas_call(
        _flash_kernel,
        grid=(HEADS // hg, SEQ // tq),
        in_specs=[
            pl.BlockSpec((hg, tq, QK_HEAD), lambda h, i: (h, i, 0)),
            pl.BlockSpec((hg, SEQ // tk, QK_HEAD, tk), lambda h, i: (h, 0, 0, 0), pipeline_mode=pl.Buffered(1)),
            pl.BlockSpec((hg, SEQ, HEAD_DIM), lambda h, i: (h, 0, 0), pipeline_mode=pl.Buffered(1)),
        ],
        out_specs=pl.BlockSpec((tq, hg * HEAD_DIM), lambda h, i: (i, h)),
        out_shape=jax.ShapeDtypeStruct((SEQ, WIDTH), BF16),
        scratch_shapes=[
            pltpu.VMEM((hg, tq, LANES), F32),
            pltpu.VMEM((hg, tq, LANES), F32),
            pltpu.VMEM((hg, tq, HEAD_DIM), F32),
        ],
        compiler_params=_params("arbitrary", "arbitrary"),
        name="mla_flash",
    )(q, kt, v)


def _gelu(x):
    return jax.nn.gelu(x)


def _gmlp_kernel(x_ref, g_ref, wu_ref, wv_ref, lng_ref, lnb_ref, ws_ref, bst_ref, o_ref):
    tm = x_ref.shape[0]
    h = _rms(x_ref[...], g_ref[...]).astype(BF16)
    u = _gelu(_dot(h, wu_ref[...]))
    gv = _gelu(_dot(h, wv_ref[...]))
    mu = jnp.mean(gv, axis=-1, keepdims=True)
    var = jnp.mean(jnp.square(gv - mu), axis=-1, keepdims=True)
    vn = ((gv - mu) * lax.rsqrt(var + EPS) * lng_ref[...] + lnb_ref[...]).astype(BF16)
    row = lax.broadcasted_iota(jnp.int32, (CHUNK, CHUNK), 0)
    col = lax.broadcasted_iota(jnp.int32, (CHUNK, CHUNK), 1)
    bst = bst_ref[...]
    for hd in range(HEADS):
        w = jnp.where(col <= row, ws_ref[hd], 0.0).astype(BF16)
        bias = bst[:, hd:hd + 1]
        cs = slice(hd * HEAD_DIM, (hd + 1) * HEAD_DIM)
        for c in range(tm // CHUNK):
            rs = slice(c * CHUNK, (c + 1) * CHUNK)
            gate = _dot(w, vn[rs, cs]) + bias
            o_ref[rs, cs] = (u[rs, cs] * gate).astype(BF16)


def _gmlp(x, g, wu, wv, lng, lnb, ws, bst):
    tm = ROW_TILE
    return pl.pallas_call(
        _gmlp_kernel,
        grid=(SEQ // tm,),
        in_specs=[
            pl.BlockSpec((tm, D_MODEL), lambda i: (i, 0)),
            _const_spec(g.shape), _const_spec(wu.shape), _const_spec(wv.shape),
            _const_spec(lng.shape), _const_spec(lnb.shape), _const_spec(ws.shape), _const_spec(bst.shape),
        ],
        out_specs=pl.BlockSpec((tm, WIDTH), lambda i: (i, 0)),
        out_shape=jax.ShapeDtypeStruct((SEQ, WIDTH), BF16),
        compiler_params=_params("arbitrary"),
        name="gmlp",
    )(x, g, wu, wv, lng, lnb, ws, bst)


def _outproj_kernel(x_ref, a_ref, b_ref, wa_ref, wb_ref, o_ref):
    o_ref[...] = x_ref[...] + _dot(a_ref[...], wa_ref[...]) + _dot(b_ref[...], wb_ref[...])


def _outproj(x, a, b, w):
    tm = ROW_TILE
    return pl.pallas_call(
        _outproj_kernel,
        grid=(SEQ // tm,),
        in_specs=[
            pl.BlockSpec((tm, D_MODEL), lambda i: (i, 0)),
            pl.BlockSpec((tm, WIDTH), lambda i: (i, 0)),
            pl.BlockSpec((tm, WIDTH), lambda i: (i, 0)),
            pl.BlockSpec((WIDTH, D_MODEL), lambda i: (0, 0), pipeline_mode=pl.Buffered(1)),
            pl.BlockSpec((WIDTH, D_MODEL), lambda i: (1, 0), pipeline_mode=pl.Buffered(1)),
        ],
        out_specs=pl.BlockSpec((tm, D_MODEL), lambda i: (i, 0)),
        out_shape=jax.ShapeDtypeStruct((SEQ, D_MODEL), F32),
        compiler_params=_params("arbitrary"),
        name="outproj0",
    )(x, a, b, w, w)


def _ffn_kernel(x_ref, g_ref, wg_ref, wu_ref, wd_ref, o_ref, h_sc):
    @pl.when(pl.program_id(1) == 0)
    def _():
        x = x_ref[...]
        h_sc[...] = _rms(x, g_ref[...]).astype(BF16)
        o_ref[...] = x

    h = h_sc[...]
    half = wg_ref.shape[1] // 2
    acc = None
    for s in range(2):
        cs = slice(s * half, (s + 1) * half)
        a = _dot(h, wg_ref[:, cs])
        b = _dot(h, wu_ref[:, cs])
        t = (a * jax.nn.sigmoid(a) * b).astype(BF16)
        part = _dot(t, wd_ref[cs, :])
        acc = part if acc is None else acc + part
    o_ref[...] += acc


def _ffn(x, g, wg, wu, wd, layer):
    tm, tf = FFN_ROW_TILE, FF_TILE
    return pl.pallas_call(
        _ffn_kernel,
        grid=(SEQ // tm, D_FF // tf),
        in_specs=[
            pl.BlockSpec((tm, D_MODEL), lambda i, j: (i, 0)),
            _const_spec(g.shape),
            pl.BlockSpec((None, D_MODEL, tf), lambda i, j: (layer, 0, j)),
            pl.BlockSpec((None, D_MODEL, tf), lambda i, j: (layer, 0, j)),
            pl.BlockSpec((None, tf, D_MODEL), lambda i, j: (layer, j, 0)),
        ],
        out_specs=pl.BlockSpec((tm, D_MODEL), lambda i, j: (i, 0)),
        out_shape=jax.ShapeDtypeStruct((SEQ, D_MODEL), F32),
        scratch_shapes=[pltpu.VMEM((tm, D_MODEL), BF16)],
        compiler_params=_params("arbitrary", "arbitrary"),
        name="ffn",
    )(x, g, wg, wu, wd)


def _conv_kernel(x_ref, g_ref, wb_ref, wc_ref, wh_ref, cw_ref, wo_ref, o_ref, h_sc, tail_sc):
    i = pl.program_id(0)
    j = pl.program_id(1)

    @pl.when(j == 0)
    def _():
        x = x_ref[...]
        h_sc[...] = _rms(x, g_ref[...]).astype(BF16)
        o_ref[...] = x

    @pl.when(i == 0)
    def _():
        tail_sc[j] = jnp.zeros(tail_sc.shape[1:], F32)

    h = h_sc[...]
    tail = tail_sc[j]
    cw = cw_ref[...]
    half = wb_ref.shape[1] // 2
    acc = None
    for s in range(2):
        cs = slice(s * half, (s + 1) * half)
        gb = _dot(h, wb_ref[:, cs])
        cz = _dot(h, wc_ref[:, cs]) * _dot(h, wh_ref[:, cs])
        prev1 = tail[7:8, cs]
        prev2 = tail[6:7, cs]
        row = lax.broadcasted_iota(jnp.int32, cz.shape, 0)
        z1 = jnp.where(row == 0, prev1, pltpu.roll(cz, 1, axis=0))
        z2 = jnp.where(row == 0, prev2, jnp.where(row == 1, prev1, pltpu.roll(cz, 2, axis=0)))
        y = cw[0:1, cs] * z2 + cw[1:2, cs] * z1 + cw[2:3, cs] * cz
        tail_sc[j, :, cs] = cz[cz.shape[0] - 8:, :]
        part = _dot((gb * y).astype(BF16), wo_ref[cs, :])
        acc = part if acc is None else acc + part
    o_ref[...] += acc


def _conv_mixer(x, g, win, cw, wo):
    tm, tc = ROW_TILE, FF_TILE
    nc = D_MODEL // tc
    return pl.pallas_call(
        _conv_kernel,
        grid=(SEQ // tm, nc),
        in_specs=[
            pl.BlockSpec((tm, D_MODEL), lambda i, j: (i, 0)),
            _const_spec(g.shape),
            pl.BlockSpec((D_MODEL, tc), lambda i, j: (0, j)),
            pl.BlockSpec((D_MODEL, tc), lambda i, j: (0, j + nc)),
            pl.BlockSpec((D_MODEL, tc), lambda i, j: (0, j + 2 * nc)),
            pl.BlockSpec((8, tc), lambda i, j: (0, j)),
            pl.BlockSpec((tc, D_MODEL), lambda i, j: (j, 0)),
        ],
        out_specs=pl.BlockSpec((tm, D_MODEL), lambda i, j: (i, 0)),
        out_shape=jax.ShapeDtypeStruct((SEQ, D_MODEL), F32),
        scratch_shapes=[pltpu.VMEM((tm, D_MODEL), BF16), pltpu.VMEM((nc, 8, tc), F32)],
        compiler_params=_params("arbitrary", "arbitrary"),
        name="conv_mixer",
    )(x, g, win, win, win, cw, wo)


def _ple_kernel(x_ref, p_ref, g_ref, wg_ref, wp_ref, gf_ref, o_ref, *, final_norm):
    x = x_ref[...]
    h = _rms(x, g_ref[...]).astype(BF16)
    gate = jax.nn.sigmoid(_dot(h, wg_ref[...]))
    y = x + gate * _dot(p_ref[...].astype(BF16), wp_ref[...])
    if final_norm:
        y = _rms(y, gf_ref[...])
    o_ref[...] = y


def _ple(x, p, g, wg, wp, gf, layer, final_norm):
    tm = ROW_TILE
    return pl.pallas_call(
        functools.partial(_ple_kernel, final_norm=final_norm),
        grid=(SEQ // tm,),
        in_specs=[
            pl.BlockSpec((tm, D_MODEL), lambda i: (i, 0)),
            pl.BlockSpec((None, None, tm, PLE_DIM), lambda i: (layer, 0, i, 0)),
            _const_spec(g.shape),
            pl.BlockSpec((None, D_MODEL, D_MODEL), lambda i: (layer, 0, 0), pipeline_mode=pl.Buffered(1)),
            pl.BlockSpec((None, PLE_DIM, D_MODEL), lambda i: (layer, 0, 0), pipeline_mode=pl.Buffered(1)),
            _const_spec(gf.shape),
        ],
        out_specs=pl.BlockSpec((tm, D_MODEL), lambda i: (i, 0)),
        out_shape=jax.ShapeDtypeStruct((SEQ, D_MODEL), F32),
        compiler_params=_params("arbitrary"),
        name="ple",
    )(x, p, g, wg, wp, gf)


def _row(v):
    return v.reshape(1, -1)


def kernel(x, p, positions, norm_mix, norm_ffn, norm_ple, w_in0, q_norm, kv_norm, w_qb, w_kvb, v_ln_g,
           v_ln_b, w_spatial, b_spatial, w_out0, w_in1, conv_w, w_out1, w_gate, w_up, w_down, w_ple_gate,
           w_ple_proj, norm_final):
    assert x.shape == (1, SEQ, D_MODEL) and p.shape == (2, 1, SEQ, PLE_DIM)
    xs = x.reshape(SEQ, D_MODEL)
    pos = positions.reshape(SEQ, 1)

    o1, o2, o3 = Q_LORA, Q_LORA + KV_LORA, Q_LORA + KV_LORA + QK_ROPE
    w0 = w_in0[0]
    k1, k2 = w0[:, o2:o2 + ROPE_HALF], w0[:, o2 + ROPE_HALF:o3]
    win_mla = jnp.concatenate([w0[:, :o2], k1, k2, k2, k1], axis=1).astype(BF16)
    wu0 = w0[:, o3:o3 + WIDTH].astype(BF16)
    wv0 = w0[:, o3 + WIDTH:].astype(BF16)
    wq = w_qb[0].reshape(Q_LORA, HEADS, QK_HEAD)
    pe = wq[:, :, HEAD_DIM:]
    pe_swapped = jnp.concatenate([pe[:, :, ROPE_HALF:], pe[:, :, :ROPE_HALF]], axis=-1)
    wqb = jnp.concatenate([wq[:, :, :HEAD_DIM].reshape(Q_LORA, WIDTH), pe.reshape(Q_LORA, HEADS * QK_ROPE),
                           pe_swapped.reshape(Q_LORA, HEADS * QK_ROPE)], axis=1).astype(BF16)
    wkv = w_kvb[0].reshape(KV_LORA, HEADS, 2 * HEAD_DIM)
    wkt = wkv[:, :, :HEAD_DIM].reshape(KV_LORA, WIDTH).T.astype(BF16)
    wv = wkv[:, :, HEAD_DIM:].reshape(KV_LORA, WIDTH).astype(BF16)
    inv_freq = ROPE_BASE ** (-jnp.arange(ROPE_HALF, dtype=F32) / ROPE_HALF)
    inv4 = jnp.tile(inv_freq, LANES // ROPE_HALF).reshape(1, LANES)
    sgn4 = jnp.tile(jnp.concatenate([-jnp.ones(ROPE_HALF, F32), jnp.ones(ROPE_HALF, F32)]), 2).reshape(1, LANES)
    cw = jnp.concatenate([conv_w[0], jnp.zeros((8 - CONV_W, D_MODEL), F32)], axis=0)
    wg_ffn, wu_ffn, wd_ffn = w_gate.astype(BF16), w_up.astype(BF16), w_down.astype(BF16)
    wg_ple, wp_ple = w_ple_gate.astype(BF16), w_ple_proj.astype(BF16)

    q, kt, v = _mla_pre(xs, pos, _row(norm_mix[0]), win_mla, _row(q_norm[0]), _row(kv_norm[0]), wqb, wkt, wv,
                        inv4, sgn4)
    attn = _flash(q, kt, v)
    gm = _gmlp(xs, _row(norm_mix[0]), wu0, wv0, _row(v_ln_g[0]), _row(v_ln_b[0]), w_spatial[0],
               b_spatial[0].T)
    xs = _outproj(xs, attn, gm, w_out0[0].astype(BF16))
    xs = _ffn(xs, _row(norm_ffn[0]), wg_ffn, wu_ffn, wd_ffn, layer=0)
    xs = _ple(xs, p, _row(norm_ple[0]), wg_ple, wp_ple, _row(norm_final), layer=0, final_norm=False)

    xs = _conv_mixer(xs, _row(norm_mix[1]), w_in1[0].astype(BF16), cw, w_out1[0].astype(BF16))
    xs = _ffn(xs, _row(norm_ffn[1]), wg_ffn, wu_ffn, wd_ffn, layer=1)
    xs = _ple(xs, p, _row(norm_ple[1]), wg_ple, wp_ple, _row(norm_final), layer=1, final_norm=True)
    return xs.reshape(1, SEQ, D_MODEL)
```

```python
import functools
import math
from typing import NamedTuple

import jax
import jax.numpy as jnp
from jax import lax
from jax.experimental import pallas as pl
from jax.experimental.pallas import tpu as pltpu

F32 = jnp.float32
BF16 = jnp.bfloat16

D_MODEL = 2048
SEQ = 16384
EPS = 1e-6
PLE_DIM = 256
HEADS = 8
HEAD_DIM = 128
CHUNK = 128
Q_LORA = 512
KV_LORA = 256
QK_ROPE = 64
ROPE_HALF = QK_ROPE // 2
QK_HEAD = HEAD_DIM + QK_ROPE
WIDTH = HEADS * HEAD_DIM
ROPE_BASE = 10000.0
D_FF = 5632
CONV_W = 3

VMEM_LIMIT_BYTES = 56 * 1024 * 1024
LANES = 128

ROW_TILE = 512
FFN_ROW_TILE = 1024
ATTN_TILE = 512
Q_TILE = 2048
HEAD_GROUP = 2
FF_TILE = 512
Q_SCALE = (QK_HEAD ** -0.5) * math.log2(math.e)
MASK_VALUE = -1e30


def _params(*semantics):
    return pltpu.CompilerParams(dimension_semantics=semantics, vmem_limit_bytes=VMEM_LIMIT_BYTES)


def _const_spec(shape):
    zeros = (0,) * len(shape)
    return pl.BlockSpec(shape, lambda *_: zeros, pipeline_mode=pl.Buffered(1))


class _Cast(NamedTuple):
    src: jax.Array
    in_spec: pl.BlockSpec
    out_spec: pl.BlockSpec
    out_shape: jax.ShapeDtypeStruct


def _with_casts(body, n_in, n_out, n_cast):
    def kern(*refs):
        ins = refs[:n_in]
        cast_ins = refs[n_in:n_in + n_cast]
        outs = refs[n_in + n_cast:n_in + n_cast + n_out]
        cast_outs = refs[n_in + n_cast + n_out:n_in + 2 * n_cast + n_out]
        scratch = refs[n_in + 2 * n_cast + n_out:]
        for s, d in zip(cast_ins, cast_outs):
            d[...] = s[...].astype(BF16)
        body(*ins, *outs, *scratch)
    return kern


def _call_with_casts(body, inputs, in_specs, out_specs, out_shape, casts, **kwargs):
    casts = tuple(casts)
    n_out = len(out_specs)
    res = pl.pallas_call(
        _with_casts(body, len(inputs), n_out, len(casts)),
        in_specs=list(in_specs) + [c.in_spec for c in casts],
        out_specs=list(out_specs) + [c.out_spec for c in casts],
        out_shape=list(out_shape) + [c.out_shape for c in casts],
        **kwargs,
    )(*inputs, *[c.src for c in casts])
    return res[:n_out], res[n_out:]


def _rms(x, g):
    return x * lax.rsqrt(jnp.mean(x * x, axis=-1, keepdims=True) + EPS) * g


def _dot(a, b):
    return jnp.dot(a, b, preferred_element_type=F32)


def _dot_nt(a, b):
    return lax.dot_general(a, b, (((1,), (1,)), ((), ())), preferred_element_type=F32)


def _mla_pre_kernel(x_ref, pos_ref, g_ref, win_ref, qn_ref, kvn_ref, wqb_ref, wkt_ref, wv_ref,
                    inv_ref, sgn_ref, q_ref, kt_ref, v_ref):
    h = _rms(x_ref[...], g_ref[...]).astype(BF16)
    z = _dot(h, win_ref[...])
    q_lat = z[:, :Q_LORA]
    kv_lat = z[:, Q_LORA:Q_LORA + KV_LORA]
    kk = z[:, Q_LORA + KV_LORA:]

    ang = pos_ref[...].astype(F32) * inv_ref[...]
    cos = jnp.cos(ang)
    sin = jnp.sin(ang) * sgn_ref[...]

    qn = _rms(q_lat, qn_ref[...]).astype(BF16)
    qf = _dot(qn, wqb_ref[...])
    cos4 = jnp.concatenate([cos] * 4, axis=-1)
    sin4 = jnp.concatenate([sin] * 4, axis=-1)
    q_pe = qf[:, WIDTH:WIDTH + 512] * cos4 + qf[:, WIDTH + 512:] * sin4
    for hd in range(HEADS):
        q_ref[hd, :, 0:HEAD_DIM] = (qf[:, hd * HEAD_DIM:(hd + 1) * HEAD_DIM] * Q_SCALE).astype(BF16)
        q_ref[hd, :, HEAD_DIM:QK_HEAD] = (q_pe[:, hd * QK_ROPE:(hd + 1) * QK_ROPE] * Q_SCALE).astype(BF16)

    lane = lax.broadcasted_iota(jnp.int32, ang.shape, 1)
    t = kk * jnp.where(lane < QK_ROPE, cos, sin)
    k_pe = t + pltpu.roll(t, QK_ROPE, axis=1)
    k_pe_t = k_pe.T[:QK_ROPE, :].astype(BF16)

    kvn = _rms(kv_lat, kvn_ref[...]).astype(BF16)
    kn_t = _dot_nt(wkt_ref[...], kvn)
    vv = _dot(kvn, wv_ref[...])
    for hd in range(HEADS):
        kt_ref[hd, 0, 0:HEAD_DIM, :] = kn_t[hd * HEAD_DIM:(hd + 1) * HEAD_DIM, :].astype(BF16)
        kt_ref[hd, 0, HEAD_DIM:QK_HEAD, :] = k_pe_t
        v_ref[hd] = vv[:, hd * HEAD_DIM:(hd + 1) * HEAD_DIM].astype(BF16)


def _mla_pre(x, pos, g, win, qn, kvn, wqb, wkt, wv, inv4, sgn4, casts=()):
    tm = ATTN_TILE
    n = SEQ // tm
    return _call_with_casts(
        _mla_pre_kernel,
        (x, pos, g, win, qn, kvn, wqb, wkt, wv, inv4, sgn4),
        grid=(n,),
        in_specs=[
            pl.BlockSpec((tm, D_MODEL), lambda i: (i, 0)),
            pl.BlockSpec((tm, 1), lambda i: (i, 0)),
            _const_spec(g.shape), _const_spec(win.shape), _const_spec(qn.shape), _const_spec(kvn.shape),
            _const_spec(wqb.shape), _const_spec(wkt.shape), _const_spec(wv.shape),
            _const_spec(inv4.shape), _const_spec(sgn4.shape),
        ],
        out_specs=[
            pl.BlockSpec((HEADS, tm, QK_HEAD), lambda i: (0, i, 0)),
            pl.BlockSpec((HEADS, 1, QK_HEAD, tm), lambda i: (0, i, 0, 0)),
            pl.BlockSpec((HEADS, tm, HEAD_DIM), lambda i: (0, i, 0)),
        ],
        out_shape=[
            jax.ShapeDtypeStruct((HEADS, SEQ, QK_HEAD), BF16),
            jax.ShapeDtypeStruct((HEADS, n, QK_HEAD, tm), BF16),
            jax.ShapeDtypeStruct((HEADS, SEQ, HEAD_DIM), BF16),
        ],
        casts=casts,
        compiler_params=_params("arbitrary"),
        name="mla_pre",
    )


def _flash_kernel(q_ref, kt_ref, v_ref, o_ref, m_sc, l_sc, acc_sc):
    qi = pl.program_id(1)
    tq, tk = Q_TILE, ATTN_TILE
    ratio = tq // tk
    ncol = tk // LANES
    m_sc[...] = jnp.full(m_sc.shape, MASK_VALUE, F32)
    l_sc[...] = jnp.zeros(l_sc.shape, F32)
    acc_sc[...] = jnp.zeros(acc_sc.shape, F32)

    def step(j, diag):
        start = pl.multiple_of(j * tk, tk)
        rows = slice(None) if diag is None else slice(diag * tk, tq)
        for g in range(HEAD_GROUP):
            s = _dot(q_ref[g, rows, :], kt_ref[g, j])
            if diag is not None:
                row = lax.broadcasted_iota(jnp.int32, s.shape, 0)
                col = lax.broadcasted_iota(jnp.int32, s.shape, 1)
                s = jnp.where(col <= row, s, MASK_VALUE)
            cols = [s[:, c * LANES:(c + 1) * LANES] for c in range(ncol)]
            m_prev = m_sc[g, rows, :]
            m_cur = functools.reduce(jnp.maximum, cols)
            m_new = jnp.maximum(m_prev, jnp.max(m_cur, axis=-1, keepdims=True))
            alpha = jnp.exp2(m_prev - m_new)
            ps = [jnp.exp2((c - m_new).astype(BF16)) for c in cols]
            l_sc[g, rows, :] = alpha * l_sc[g, rows, :] + functools.reduce(jnp.add, ps).astype(F32)
            p = jnp.concatenate(ps, axis=-1)
            acc_sc[g, rows, :] = alpha * acc_sc[g, rows, :] + _dot(p, v_ref[g, pl.ds(start, tk), :])
            m_sc[g, rows, :] = m_new

    def body(j, carry):
        step(j, None)
        return carry

    lax.fori_loop(0, qi * ratio, body, 0)
    for d in range(ratio):
        step(qi * ratio + d, d)
    for g in range(HEAD_GROUP):
        l = jnp.sum(l_sc[g], axis=-1, keepdims=True)
        o_ref[:, g * HEAD_DIM:(g + 1) * HEAD_DIM] = (acc_sc[g] / l).astype(BF16)


def _flash(q, kt, v):
    tq, tk = Q_TILE, ATTN_TILE
    hg = HEAD_GROUP
    return pl.pallas_call(
        _flash_kernel,
        grid=(HEADS // hg, SEQ // tq),
        in_specs=[
            pl.BlockSpec((hg, tq, QK_HEAD), lambda h, i: (h, i, 0)),
            pl.BlockSpec((hg, SEQ // tk, QK_HEAD, tk), lambda h, i: (h, 0, 0, 0), pipeline_mode=pl.Buffered(1)),
            pl.BlockSpec((hg, SEQ, HEAD_DIM), lambda h, i: (h, 0, 0), pipeline_mode=pl.Buffered(1)),
        ],
        out_specs=pl.BlockSpec((tq, hg * HEAD_DIM), lambda h, i: (i, h)),
        out_shape=jax.ShapeDtypeStruct((SEQ, WIDTH), BF16),
        scratch_shapes=[
            pltpu.VMEM((hg, tq, LANES), F32),
            pltpu.VMEM((hg, tq, LANES), F32),
            pltpu.VMEM((hg, tq, HEAD_DIM), F32),
        ],
        compiler_params=_params("arbitrary", "arbitrary"),
        name="mla_flash",
    )(q, kt, v)


def _gelu(x):
    return jax.nn.gelu(x)


def _gmlp_kernel(x_ref, g_ref, wu_ref, wv_ref, lng_ref, lnb_ref, ws_ref, bst_ref, o_ref):
    tm = x_ref.shape[0]
    h = _rms(x_ref[...], g_ref[...]).astype(BF16)
    u = _gelu(_dot(h, wu_ref[...]))
    gv = _gelu(_dot(h, wv_ref[...]))
    mu = jnp.mean(gv, axis=-1, keepdims=True)
    var = jnp.mean(jnp.square(gv - mu), axis=-1, keepdims=True)
    vn = ((gv - mu) * lax.rsqrt(var + EPS) * lng_ref[...] + lnb_ref[...]).astype(BF16)
    row = lax.broadcasted_iota(jnp.int32, (CHUNK, CHUNK), 0)
    col = lax.broadcasted_iota(jnp.int32, (CHUNK, CHUNK), 1)
    bst = bst_ref[...]
    for hd in range(HEADS):
        w = jnp.where(col <= row, ws_ref[hd], 0.0).astype(BF16)
        bias = bst[:, hd:hd + 1]
        cs = slice(hd * HEAD_DIM, (hd + 1) * HEAD_DIM)
        for c in range(tm // CHUNK):
            rs = slice(c * CHUNK, (c + 1) * CHUNK)
            gate = _dot(w, vn[rs, cs]) + bias
            o_ref[rs, cs] = (u[rs, cs] * gate).astype(BF16)


def _gmlp(x, g, wu, wv, lng, lnb, ws, bst, casts=()):
    tm = ROW_TILE
    return _call_with_casts(
        _gmlp_kernel,
        (x, g, wu, wv, lng, lnb, ws, bst),
        grid=(SEQ // tm,),
        in_specs=[
            pl.BlockSpec((tm, D_MODEL), lambda i: (i, 0)),
            _const_spec(g.shape), _const_spec(wu.shape), _const_spec(wv.shape),
            _const_spec(lng.shape), _const_spec(lnb.shape), _const_spec(ws.shape), _const_spec(bst.shape),
        ],
        out_specs=[pl.BlockSpec((tm, WIDTH), lambda i: (i, 0))],
        out_shape=[jax.ShapeDtypeStruct((SEQ, WIDTH), BF16)],
        casts=casts,
        compiler_params=_params("arbitrary"),
        name="gmlp",
    )


def _outproj_kernel(x_ref, a_ref, b_ref, wa_ref, wb_ref, o_ref):
    o_ref[...] = x_ref[...] + _dot(a_ref[...], wa_ref[...]) + _dot(b_ref[...], wb_ref[...])


def _outproj(x, a, b, w, casts=()):
    tm = ROW_TILE
    return _call_with_casts(
        _outproj_kernel,
        (x, a, b, w, w),
        grid=(SEQ // tm,),
        in_specs=[
            pl.BlockSpec((tm, D_MODEL), lambda i: (i, 0)),
            pl.BlockSpec((tm, WIDTH), lambda i: (i, 0)),
            pl.BlockSpec((tm, WIDTH), lambda i: (i, 0)),
            pl.BlockSpec((WIDTH, D_MODEL), lambda i: (0, 0), pipeline_mode=pl.Buffered(1)),
            pl.BlockSpec((WIDTH, D_MODEL), lambda i: (1, 0), pipeline_mode=pl.Buffered(1)),
        ],
        out_specs=[pl.BlockSpec((tm, D_MODEL), lambda i: (i, 0))],
        out_shape=[jax.ShapeDtypeStruct((SEQ, D_MODEL), F32)],
        casts=casts,
        compiler_params=_params("arbitrary"),
        name="outproj0",
    )


def _ffn_kernel(x_ref, g_ref, wg_ref, wu_ref, wd_ref, o_ref, h_sc):
    @pl.when(pl.program_id(1) == 0)
    def _():
        x = x_ref[...]
        h_sc[...] = _rms(x, g_ref[...]).astype(BF16)
        o_ref[...] = x

    h = h_sc[...]
    half = wg_ref.shape[1] // 2
    acc = None
    for s in range(2):
        cs = slice(s * half, (s + 1) * half)
        a = _dot(h, wg_ref[:, cs])
        b = _dot(h, wu_ref[:, cs])
        t = (a * jax.nn.sigmoid(a) * b).astype(BF16)
        part = _dot(t, wd_ref[cs, :])
        acc = part if acc is None else acc + part
    o_ref[...] += acc


def _ffn(x, g, wg, wu, wd):
    tm, tf = FFN_ROW_TILE, FF_TILE
    return pl.pallas_call(
        _ffn_kernel,
        grid=(SEQ // tm, D_FF // tf),
        in_specs=[
            pl.BlockSpec((tm, D_MODEL), lambda i, j: (i, 0)),
            _const_spec(g.shape),
            pl.BlockSpec((D_MODEL, tf), lambda i, j: (0, j)),
            pl.BlockSpec((D_MODEL, tf), lambda i, j: (0, j)),
            pl.BlockSpec((tf, D_MODEL), lambda i, j: (j, 0)),
        ],
        out_specs=pl.BlockSpec((tm, D_MODEL), lambda i, j: (i, 0)),
        out_shape=jax.ShapeDtypeStruct((SEQ, D_MODEL), F32),
        scratch_shapes=[pltpu.VMEM((tm, D_MODEL), BF16)],
        compiler_params=_params("arbitrary", "arbitrary"),
        name="ffn",
    )(x, g, wg, wu, wd)


def _conv_kernel(x_ref, g_ref, wb_ref, wc_ref, wh_ref, cw_ref, wo_ref, o_ref, h_sc, tail_sc):
    i = pl.program_id(0)
    j = pl.program_id(1)

    @pl.when(j == 0)
    def _():
        x = x_ref[...]
        h_sc[...] = _rms(x, g_ref[...]).astype(BF16)
        o_ref[...] = x

    @pl.when(i == 0)
    def _():
        tail_sc[j] = jnp.zeros(tail_sc.shape[1:], F32)

    h = h_sc[...]
    tail = tail_sc[j]
    cw = cw_ref[...]
    half = wb_ref.shape[1] // 2
    acc = None
    for s in range(2):
        cs = slice(s * half, (s + 1) * half)
        gb = _dot(h, wb_ref[:, cs])
        cz = _dot(h, wc_ref[:, cs]) * _dot(h, wh_ref[:, cs])
        prev1 = tail[7:8, cs]
        prev2 = tail[6:7, cs]
        row = lax.broadcasted_iota(jnp.int32, cz.shape, 0)
        z1 = jnp.where(row == 0, prev1, pltpu.roll(cz, 1, axis=0))
        z2 = jnp.where(row == 0, prev2, jnp.where(row == 1, prev1, pltpu.roll(cz, 2, axis=0)))
        y = cw[0:1, cs] * z2 + cw[1:2, cs] * z1 + cw[2:3, cs] * cz
        tail_sc[j, :, cs] = cz[cz.shape[0] - 8:, :]
        part = _dot((gb * y).astype(BF16), wo_ref[cs, :])
        acc = part if acc is None else acc + part
    o_ref[...] += acc


def _conv_mixer(x, g, win, cw, wo, casts=()):
    tm, tc = ROW_TILE, FF_TILE
    nc = D_MODEL // tc
    return _call_with_casts(
        _conv_kernel,
        (x, g, win, win, win, cw, wo),
        grid=(SEQ // tm, nc),
        in_specs=[
            pl.BlockSpec((tm, D_MODEL), lambda i, j: (i, 0)),
            _const_spec(g.shape),
            pl.BlockSpec((D_MODEL, tc), lambda i, j: (0, j)),
            pl.BlockSpec((D_MODEL, tc), lambda i, j: (0, j + nc)),
            pl.BlockSpec((D_MODEL, tc), lambda i, j: (0, j + 2 * nc)),
            pl.BlockSpec((8, tc), lambda i, j: (0, j)),
            pl.BlockSpec((tc, D_MODEL), lambda i, j: (j, 0)),
        ],
        out_specs=[pl.BlockSpec((tm, D_MODEL), lambda i, j: (i, 0))],
        out_shape=[jax.ShapeDtypeStruct((SEQ, D_MODEL), F32)],
        casts=casts,
        scratch_shapes=[pltpu.VMEM((tm, D_MODEL), BF16), pltpu.VMEM((nc, 8, tc), F32)],
        compiler_params=_params("arbitrary", "arbitrary"),
        name="conv_mixer",
    )


def _ple_kernel(x_ref, p_ref, g_ref, wg_ref, wp_ref, gf_ref, o_ref, *, final_norm):
    x = x_ref[...]
    h = _rms(x, g_ref[...]).astype(BF16)
    gate = jax.nn.sigmoid(_dot(h, wg_ref[...]))
    y = x + gate * _dot(p_ref[...].astype(BF16), wp_ref[...])
    if final_norm:
        y = _rms(y, gf_ref[...])
    o_ref[...] = y


def _ple(x, p, g, wg, wp, gf, layer, final_norm):
    tm = ROW_TILE
    return pl.pallas_call(
        functools.partial(_ple_kernel, final_norm=final_norm),
        grid=(SEQ // tm,),
        in_specs=[
            pl.BlockSpec((tm, D_MODEL), lambda i: (i, 0)),
            pl.BlockSpec((None, None, tm, PLE_DIM), lambda i: (layer, 0, i, 0)),
            _const_spec(g.shape),
            pl.BlockSpec((None, D_MODEL, D_MODEL), lambda i: (layer, 0, 0), pipeline_mode=pl.Buffered(1)),
            pl.BlockSpec((None, PLE_DIM, D_MODEL), lambda i: (layer, 0, 0), pipeline_mode=pl.Buffered(1)),
            _const_spec(gf.shape),
        ],
        out_specs=pl.BlockSpec((tm, D_MODEL), lambda i: (i, 0)),
        out_shape=jax.ShapeDtypeStruct((SEQ, D_MODEL), F32),
        compiler_params=_params("arbitrary"),
        name="ple",
    )(x, p, g, wg, wp, gf)


def _row(v):
    return v.reshape(1, -1)


def kernel(x, p, positions, norm_mix, norm_ffn, norm_ple, w_in0, q_norm, kv_norm, w_qb, w_kvb, v_ln_g,
           v_ln_b, w_spatial, b_spatial, w_out0, w_in1, conv_w, w_out1, w_gate, w_up, w_down, w_ple_gate,
           w_ple_proj, norm_final):
    assert x.shape == (1, SEQ, D_MODEL) and p.shape == (2, 1, SEQ, PLE_DIM)
    xs = x.reshape(SEQ, D_MODEL)
    pos = positions.reshape(SEQ, 1)

    o1, o2, o3 = Q_LORA, Q_LORA + KV_LORA, Q_LORA + KV_LORA + QK_ROPE
    w0 = w_in0[0]
    k1, k2 = w0[:, o2:o2 + ROPE_HALF], w0[:, o2 + ROPE_HALF:o3]
    win_mla = jnp.concatenate([w0[:, :o2], k1, k2, k2, k1], axis=1).astype(BF16)
    wu0 = w0[:, o3:o3 + WIDTH].astype(BF16)
    wv0 = w0[:, o3 + WIDTH:].astype(BF16)
    wq = w_qb[0].reshape(Q_LORA, HEADS, QK_HEAD)
    pe = wq[:, :, HEAD_DIM:]
    pe_swapped = jnp.concatenate([pe[:, :, ROPE_HALF:], pe[:, :, :ROPE_HALF]], axis=-1)
    wqb = jnp.concatenate([wq[:, :, :HEAD_DIM].reshape(Q_LORA, WIDTH), pe.reshape(Q_LORA, HEADS * QK_ROPE),
                           pe_swapped.reshape(Q_LORA, HEADS * QK_ROPE)], axis=1).astype(BF16)
    wkv = w_kvb[0].reshape(KV_LORA, HEADS, 2 * HEAD_DIM)
    wkt = wkv[:, :, :HEAD_DIM].reshape(KV_LORA, WIDTH).T.astype(BF16)
    wv = wkv[:, :, HEAD_DIM:].reshape(KV_LORA, WIDTH).astype(BF16)
    inv_freq = ROPE_BASE ** (-jnp.arange(ROPE_HALF, dtype=F32) / ROPE_HALF)
    inv4 = jnp.tile(inv_freq, LANES // ROPE_HALF).reshape(1, LANES)
    sgn4 = jnp.tile(jnp.concatenate([-jnp.ones(ROPE_HALF, F32), jnp.ones(ROPE_HALF, F32)]), 2).reshape(1, LANES)
    cw = jnp.concatenate([conv_w[0], jnp.zeros((8 - CONV_W, D_MODEL), F32)], axis=0)
    wp_ple = w_ple_proj.astype(BF16)

    n_attn, n_row = SEQ // ATTN_TILE, SEQ // ROW_TILE

    def slab_cast(w, layer, steps):
        rows, cols = w.shape[1] // steps, w.shape[2]
        return _Cast(w, pl.BlockSpec((None, rows, cols), lambda i: (layer, i, 0)),
                     pl.BlockSpec((rows, cols), lambda i: (i, 0)),
                     jax.ShapeDtypeStruct(w.shape[1:], BF16))

    def tile_cast(w, layer):
        blk = (w.shape[1] // n_row, w.shape[2] // (D_MODEL // FF_TILE))
        return _Cast(w, pl.BlockSpec((None,) + blk, lambda i, j: (layer, i, j)),
                     pl.BlockSpec(blk, lambda i, j: (i, j)), jax.ShapeDtypeStruct(w.shape[1:], BF16))

    ple_rows = D_MODEL // n_row
    ple_gate_cast = _Cast(w_ple_gate, pl.BlockSpec((2, ple_rows, D_MODEL), lambda i: (0, i, 0)),
                          pl.BlockSpec((2, ple_rows, D_MODEL), lambda i: (0, i, 0)),
                          jax.ShapeDtypeStruct(w_ple_gate.shape, BF16))

    (q, kt, v), (wg0, wu0_ffn) = _mla_pre(
        xs, pos, _row(norm_mix[0]), win_mla, _row(q_norm[0]), _row(kv_norm[0]), wqb, wkt, wv, inv4, sgn4,
        casts=[slab_cast(w_gate, 0, n_attn), slab_cast(w_up, 0, n_attn)])
    attn = _flash(q, kt, v)
    (gm,), (wd0, win1) = _gmlp(
        xs, _row(norm_mix[0]), wu0, wv0, _row(v_ln_g[0]), _row(v_ln_b[0]), w_spatial[0], b_spatial[0].T,
        casts=[slab_cast(w_down, 0, n_row), slab_cast(w_in1, 0, n_row)])
    (xs,), (wo1, wg_ple) = _outproj(xs, attn, gm, w_out0[0].astype(BF16),
                                    casts=[slab_cast(w_out1, 0, n_row), ple_gate_cast])
    xs = _ffn(xs, _row(norm_ffn[0]), wg0, wu0_ffn, wd0)
    xs = _ple(xs, p, _row(norm_ple[0]), wg_ple, wp_ple, _row(norm_final), layer=0, final_norm=False)

    (xs,), (wg1, wu1, wd1) = _conv_mixer(
        xs, _row(norm_mix[1]), win1, cw, wo1,
        casts=[tile_cast(w_gate, 1), tile_cast(w_up, 1), tile_cast(w_down, 1)])
    xs = _ffn(xs, _row(norm_ffn[1]), wg1, wu1, wd1)
    xs = _ple(xs, p, _row(norm_ple[1]), wg_ple, wp_ple, _row(norm_final), layer=1, final_norm=True)
    return xs.reshape(1, SEQ, D_MODEL)
```

```python
import functools
import math
from typing import NamedTuple

import jax
import jax.numpy as jnp
from jax import lax
from jax.experimental import pallas as pl
from jax.experimental.pallas import tpu as pltpu

F32 = jnp.float32
BF16 = jnp.bfloat16

D_MODEL = 2048
SEQ = 16384
EPS = 1e-6
PLE_DIM = 256
HEADS = 8
HEAD_DIM = 128
CHUNK = 128
Q_LORA = 512
KV_LORA = 256
QK_ROPE = 64
ROPE_HALF = QK_ROPE // 2
QK_HEAD = HEAD_DIM + QK_ROPE
WIDTH = HEADS * HEAD_DIM
ROPE_BASE = 10000.0
D_FF = 5632
CONV_W = 3

VMEM_LIMIT_BYTES = 56 * 1024 * 1024
LANES = 128

ROW_TILE = 512
FFN_ROW_TILE = 1024
ATTN_TILE = 512
Q_TILE = 1024
HEAD_GROUP = 4
PLE_ROW_TILE = 1024
FF_TILE = 512
Q_SCALE = (QK_HEAD ** -0.5) * math.log2(math.e)
MASK_VALUE = -1e30


def _params(*semantics):
    return pltpu.CompilerParams(dimension_semantics=semantics, vmem_limit_bytes=VMEM_LIMIT_BYTES)


def _const_spec(shape):
    zeros = (0,) * len(shape)
    return pl.BlockSpec(shape, lambda *_: zeros, pipeline_mode=pl.Buffered(1))


class _Cast(NamedTuple):
    src: jax.Array
    in_spec: pl.BlockSpec
    out_spec: pl.BlockSpec
    out_shape: jax.ShapeDtypeStruct


def _with_casts(body, n_in, n_out, n_cast):
    def kern(*refs):
        ins = refs[:n_in]
        cast_ins = refs[n_in:n_in + n_cast]
        outs = refs[n_in + n_cast:n_in + n_cast + n_out]
        cast_outs = refs[n_in + n_cast + n_out:n_in + 2 * n_cast + n_out]
        scratch = refs[n_in + 2 * n_cast + n_out:]
        for s, d in zip(cast_ins, cast_outs):
            d[...] = s[...].astype(BF16)
        body(*ins, *outs, *scratch)
    return kern


def _call_with_casts(body, inputs, in_specs, out_specs, out_shape, casts, **kwargs):
    casts = tuple(casts)
    n_out = len(out_specs)
    res = pl.pallas_call(
        _with_casts(body, len(inputs), n_out, len(casts)),
        in_specs=list(in_specs) + [c.in_spec for c in casts],
        out_specs=list(out_specs) + [c.out_spec for c in casts],
        out_shape=list(out_shape) + [c.out_shape for c in casts],
        **kwargs,
    )(*inputs, *[c.src for c in casts])
    return res[:n_out], res[n_out:]


def _rms(x, g):
    return x * lax.rsqrt(jnp.mean(x * x, axis=-1, keepdims=True) + EPS) * g


def _dot(a, b):
    return jnp.dot(a, b, preferred_element_type=F32)


def _dot_nt(a, b):
    return lax.dot_general(a, b, (((1,), (1,)), ((), ())), preferred_element_type=F32)


def _mla_pre_kernel(x_ref, pos_ref, g_ref, win_ref, qn_ref, kvn_ref, wqb_ref, wkt_ref, wv_ref,
                    inv_ref, sgn_ref, q_ref, kt_ref, v_ref):
    h = _rms(x_ref[...], g_ref[...]).astype(BF16)
    z = _dot(h, win_ref[...])
    q_lat = z[:, :Q_LORA]
    kv_lat = z[:, Q_LORA:Q_LORA + KV_LORA]
    kk = z[:, Q_LORA + KV_LORA:]

    ang = pos_ref[...].astype(F32) * inv_ref[...]
    cos = jnp.cos(ang)
    sin = jnp.sin(ang) * sgn_ref[...]

    qn = _rms(q_lat, qn_ref[...]).astype(BF16)
    qf = _dot(qn, wqb_ref[...])
    cos4 = jnp.concatenate([cos] * 4, axis=-1)
    sin4 = jnp.concatenate([sin] * 4, axis=-1)
    q_pe = qf[:, WIDTH:WIDTH + 512] * cos4 + qf[:, WIDTH + 512:] * sin4
    for hd in range(HEADS):
        q_ref[hd, :, 0:HEAD_DIM] = (qf[:, hd * HEAD_DIM:(hd + 1) * HEAD_DIM] * Q_SCALE).astype(BF16)
        q_ref[hd, :, HEAD_DIM:QK_HEAD] = (q_pe[:, hd * QK_ROPE:(hd + 1) * QK_ROPE] * Q_SCALE).astype(BF16)

    lane = lax.broadcasted_iota(jnp.int32, ang.shape, 1)
    t = kk * jnp.where(lane < QK_ROPE, cos, sin)
    k_pe = t + pltpu.roll(t, QK_ROPE, axis=1)
    k_pe_t = k_pe.T[:QK_ROPE, :].astype(BF16)

    kvn = _rms(kv_lat, kvn_ref[...]).astype(BF16)
    kn_t = _dot_nt(wkt_ref[...], kvn)
    vv = _dot(kvn, wv_ref[...])
    for hd in range(HEADS):
        kt_ref[hd, 0, 0:HEAD_DIM, :] = kn_t[hd * HEAD_DIM:(hd + 1) * HEAD_DIM, :].astype(BF16)
        kt_ref[hd, 0, HEAD_DIM:QK_HEAD, :] = k_pe_t
        v_ref[hd] = vv[:, hd * HEAD_DIM:(hd + 1) * HEAD_DIM].astype(BF16)


def _mla_pre(x, pos, g, win, qn, kvn, wqb, wkt, wv, inv4, sgn4, casts=()):
    tm = ATTN_TILE
    n = SEQ // tm
    return _call_with_casts(
        _mla_pre_kernel,
        (x, pos, g, win, qn, kvn, wqb, wkt, wv, inv4, sgn4),
        grid=(n,),
        in_specs=[
            pl.BlockSpec((tm, D_MODEL), lambda i: (i, 0)),
            pl.BlockSpec((tm, 1), lambda i: (i, 0)),
            _const_spec(g.shape), _const_spec(win.shape), _const_spec(qn.shape), _const_spec(kvn.shape),
            _const_spec(wqb.shape), _const_spec(wkt.shape), _const_spec(wv.shape),
            _const_spec(inv4.shape), _const_spec(sgn4.shape),
        ],
        out_specs=[
            pl.BlockSpec((HEADS, tm, QK_HEAD), lambda i: (0, i, 0)),
            pl.BlockSpec((HEADS, 1, QK_HEAD, tm), lambda i: (0, i, 0, 0)),
            pl.BlockSpec((HEADS, tm, HEAD_DIM), lambda i: (0, i, 0)),
        ],
        out_shape=[
            jax.ShapeDtypeStruct((HEADS, SEQ, QK_HEAD), BF16),
            jax.ShapeDtypeStruct((HEADS, n, QK_HEAD, tm), BF16),
            jax.ShapeDtypeStruct((HEADS, SEQ, HEAD_DIM), BF16),
        ],
        casts=casts,
        compiler_params=_params("arbitrary"),
        name="mla_pre",
    )


def _flash_kernel(q_ref, kt_ref, v_ref, o_ref, m_sc, l_sc, acc_sc):
    qi = pl.program_id(1)
    tq, tk = Q_TILE, ATTN_TILE
    ratio = tq // tk
    ncol = tk // LANES
    m_sc[...] = jnp.full(m_sc.shape, MASK_VALUE, F32)
    l_sc[...] = jnp.zeros(l_sc.shape, F32)
    acc_sc[...] = jnp.zeros(acc_sc.shape, F32)

    def step(j, diag):
        start = pl.multiple_of(j * tk, tk)
        rows = slice(None) if diag is None else slice(diag * tk, tq)
        for g in range(HEAD_GROUP):
            s = _dot(q_ref[g, rows, :], kt_ref[g, j])
            if diag is not None:
                row = lax.broadcasted_iota(jnp.int32, s.shape, 0)
                col = lax.broadcasted_iota(jnp.int32, s.shape, 1)
                s = jnp.where(col <= row, s, MASK_VALUE)
            cols = [s[:, c * LANES:(c + 1) * LANES] for c in range(ncol)]
            m_prev = m_sc[g, rows, :]
            m_cur = functools.reduce(jnp.maximum, cols)
            m_new = jnp.maximum(m_prev, jnp.max(m_cur, axis=-1, keepdims=True))
            alpha = jnp.exp2(m_prev - m_new)
            ps = [jnp.exp2((c - m_new).astype(BF16)) for c in cols]
            l_sc[g, rows, :] = alpha * l_sc[g, rows, :] + functools.reduce(jnp.add, ps).astype(F32)
            p = jnp.concatenate(ps, axis=-1)
            acc_sc[g, rows, :] = alpha * acc_sc[g, rows, :] + _dot(p, v_ref[g, pl.ds(start, tk), :])
            m_sc[g, rows, :] = m_new

    def body(j, carry):
        step(j, None)
        return carry

    lax.fori_loop(0, qi * ratio, body, 0)
    for d in range(ratio):
        step(qi * ratio + d, d)
    for g in range(HEAD_GROUP):
        l = jnp.sum(l_sc[g], axis=-1, keepdims=True)
        o_ref[:, g * HEAD_DIM:(g + 1) * HEAD_DIM] = (acc_sc[g] / l).astype(BF16)


def _flash(q, kt, v):
    tq, tk = Q_TILE, ATTN_TILE
    hg = HEAD_GROUP
    return pl.pallas_call(
        _flash_kernel,
        grid=(HEADS // hg, SEQ // tq),
        in_specs=[
            pl.BlockSpec((hg, tq, QK_HEAD), lambda h, i: (h, i, 0)),
            pl.BlockSpec((hg, SEQ // tk, QK_HEAD, tk), lambda h, i: (h, 0, 0, 0), pipeline_mode=pl.Buffered(1)),
            pl.BlockSpec((hg, SEQ, HEAD_DIM), lambda h, i: (h, 0, 0), pipeline_mode=pl.Buffered(1)),
        ],
        out_specs=pl.BlockSpec((tq, hg * HEAD_DIM), lambda h, i: (i, h)),
        out_shape=jax.ShapeDtypeStruct((SEQ, WIDTH), BF16),
        scratch_shapes=[
            pltpu.VMEM((hg, tq, LANES), F32),
            pltpu.VMEM((hg, tq, LANES), F32),
            pltpu.VMEM((hg, tq, HEAD_DIM), F32),
        ],
        compiler_params=_params("arbitrary", "arbitrary"),
        name="mla_flash",
    )(q, kt, v)


def _gelu(x):
    return jax.nn.gelu(x)


def _gmlp_kernel(x_ref, g_ref, wu_ref, wv_ref, lng_ref, lnb_ref, ws_ref, bst_ref, o_ref):
    tm = x_ref.shape[0]
    h = _rms(x_ref[...], g_ref[...]).astype(BF16)
    u = _gelu(_dot(h, wu_ref[...]))
    gv = _gelu(_dot(h, wv_ref[...]))
    mu = jnp.mean(gv, axis=-1, keepdims=True)
    var = jnp.mean(jnp.square(gv - mu), axis=-1, keepdims=True)
    vn = ((gv - mu) * lax.rsqrt(var + EPS) * lng_ref[...] + lnb_ref[...]).astype(BF16)
    row = lax.broadcasted_iota(jnp.int32, (CHUNK, CHUNK), 0)
    col = lax.broadcasted_iota(jnp.int32, (CHUNK, CHUNK), 1)
    bst = bst_ref[...]
    for hd in range(HEADS):
        w = jnp.where(col <= row, ws_ref[hd], 0.0).astype(BF16)
        bias = bst[:, hd:hd + 1]
        cs = slice(hd * HEAD_DIM, (hd + 1) * HEAD_DIM)
        for c in range(tm // CHUNK):
            rs = slice(c * CHUNK, (c + 1) * CHUNK)
            gate = _dot(w, vn[rs, cs]) + bias
            o_ref[rs, cs] = (u[rs, cs] * gate).astype(BF16)


def _gmlp(x, g, wu, wv, lng, lnb, ws, bst, casts=()):
    tm = ROW_TILE
    return _call_with_casts(
        _gmlp_kernel,
        (x, g, wu, wv, lng, lnb, ws, bst),
        grid=(SEQ // tm,),
        in_specs=[
            pl.BlockSpec((tm, D_MODEL), lambda i: (i, 0)),
            _const_spec(g.shape), _const_spec(wu.shape), _const_spec(wv.shape),
            _const_spec(lng.shape), _const_spec(lnb.shape), _const_spec(ws.shape), _const_spec(bst.shape),
        ],
        out_specs=[pl.BlockSpec((tm, WIDTH), lambda i: (i, 0))],
        out_shape=[jax.ShapeDtypeStruct((SEQ, WIDTH), BF16)],
        casts=casts,
        compiler_params=_params("arbitrary"),
        name="gmlp",
    )


def _outproj_kernel(x_ref, a_ref, b_ref, wa_ref, wb_ref, o_ref):
    o_ref[...] = x_ref[...] + _dot(a_ref[...], wa_ref[...]) + _dot(b_ref[...], wb_ref[...])


def _outproj(x, a, b, w, casts=()):
    tm = ROW_TILE
    return _call_with_casts(
        _outproj_kernel,
        (x, a, b, w, w),
        grid=(SEQ // tm,),
        in_specs=[
            pl.BlockSpec((tm, D_MODEL), lambda i: (i, 0)),
            pl.BlockSpec((tm, WIDTH), lambda i: (i, 0)),
            pl.BlockSpec((tm, WIDTH), lambda i: (i, 0)),
            pl.BlockSpec((WIDTH, D_MODEL), lambda i: (0, 0), pipeline_mode=pl.Buffered(1)),
            pl.BlockSpec((WIDTH, D_MODEL), lambda i: (1, 0), pipeline_mode=pl.Buffered(1)),
        ],
        out_specs=[pl.BlockSpec((tm, D_MODEL), lambda i: (i, 0))],
        out_shape=[jax.ShapeDtypeStruct((SEQ, D_MODEL), F32)],
        casts=casts,
        compiler_params=_params("arbitrary"),
        name="outproj0",
    )


def _ffn_kernel(x_ref, g_ref, wg_ref, wu_ref, wd_ref, o_ref, h_sc):
    @pl.when(pl.program_id(1) == 0)
    def _():
        x = x_ref[...]
        h_sc[...] = _rms(x, g_ref[...]).astype(BF16)
        o_ref[...] = x

    h = h_sc[...]
    half = wg_ref.shape[1] // 2
    acc = None
    for s in range(2):
        cs = slice(s * half, (s + 1) * half)
        a = _dot(h, wg_ref[:, cs])
        b = _dot(h, wu_ref[:, cs])
        t = (a * jax.nn.sigmoid(a) * b).astype(BF16)
        part = _dot(t, wd_ref[cs, :])
        acc = part if acc is None else acc + part
    o_ref[...] += acc


def _ffn(x, g, wg, wu, wd):
    tm, tf = FFN_ROW_TILE, FF_TILE
    return pl.pallas_call(
        _ffn_kernel,
        grid=(SEQ // tm, D_FF // tf),
        in_specs=[
            pl.BlockSpec((tm, D_MODEL), lambda i, j: (i, 0)),
            _const_spec(g.shape),
            pl.BlockSpec((D_MODEL, tf), lambda i, j: (0, j)),
            pl.BlockSpec((D_MODEL, tf), lambda i, j: (0, j)),
            pl.BlockSpec((tf, D_MODEL), lambda i, j: (j, 0)),
        ],
        out_specs=pl.BlockSpec((tm, D_MODEL), lambda i, j: (i, 0)),
        out_shape=jax.ShapeDtypeStruct((SEQ, D_MODEL), F32),
        scratch_shapes=[pltpu.VMEM((tm, D_MODEL), BF16)],
        compiler_params=_params("arbitrary", "arbitrary"),
        name="ffn",
    )(x, g, wg, wu, wd)


def _conv_kernel(x_ref, g_ref, wb_ref, wc_ref, wh_ref, cw_ref, wo_ref, o_ref, h_sc, tail_sc):
    i = pl.program_id(0)
    j = pl.program_id(1)

    @pl.when(j == 0)
    def _():
        x = x_ref[...]
        h_sc[...] = _rms(x, g_ref[...]).astype(BF16)
        o_ref[...] = x

    @pl.when(i == 0)
    def _():
        tail_sc[j] = jnp.zeros(tail_sc.shape[1:], F32)

    h = h_sc[...]
    tail = tail_sc[j]
    cw = cw_ref[...]
    half = wb_ref.shape[1] // 2
    acc = None
    for s in range(2):
        cs = slice(s * half, (s + 1) * half)
        gb = _dot(h, wb_ref[:, cs])
        cz = _dot(h, wc_ref[:, cs]) * _dot(h, wh_ref[:, cs])
        prev1 = tail[7:8, cs]
        prev2 = tail[6:7, cs]
        row = lax.broadcasted_iota(jnp.int32, cz.shape, 0)
        z1 = jnp.where(row == 0, prev1, pltpu.roll(cz, 1, axis=0))
        z2 = jnp.where(row == 0, prev2, jnp.where(row == 1, prev1, pltpu.roll(cz, 2, axis=0)))
        y = cw[0:1, cs] * z2 + cw[1:2, cs] * z1 + cw[2:3, cs] * cz
        tail_sc[j, :, cs] = cz[cz.shape[0] - 8:, :]
        part = _dot((gb * y).astype(BF16), wo_ref[cs, :])
        acc = part if acc is None else acc + part
    o_ref[...] += acc


def _conv_mixer(x, g, win, cw, wo, casts=()):
    tm, tc = ROW_TILE, FF_TILE
    nc = D_MODEL // tc
    return _call_with_casts(
        _conv_kernel,
        (x, g, win, win, win, cw, wo),
        grid=(SEQ // tm, nc),
        in_specs=[
            pl.BlockSpec((tm, D_MODEL), lambda i, j: (i, 0)),
            _const_spec(g.shape),
            pl.BlockSpec((D_MODEL, tc), lambda i, j: (0, j)),
            pl.BlockSpec((D_MODEL, tc), lambda i, j: (0, j + nc)),
            pl.BlockSpec((D_MODEL, tc), lambda i, j: (0, j + 2 * nc)),
            pl.BlockSpec((8, tc), lambda i, j: (0, j)),
            pl.BlockSpec((tc, D_MODEL), lambda i, j: (j, 0)),
        ],
        out_specs=[pl.BlockSpec((tm, D_MODEL), lambda i, j: (i, 0))],
        out_shape=[jax.ShapeDtypeStruct((SEQ, D_MODEL), F32)],
        casts=casts,
        scratch_shapes=[pltpu.VMEM((tm, D_MODEL), BF16), pltpu.VMEM((nc, 8, tc), F32)],
        compiler_params=_params("arbitrary", "arbitrary"),
        name="conv_mixer",
    )


def _ple_kernel(x_ref, p_ref, g_ref, wg_ref, wp_ref, gf_ref, o_ref, *, final_norm):
    x = x_ref[...]
    h = _rms(x, g_ref[...]).astype(BF16)
    gate = jax.nn.sigmoid(_dot(h, wg_ref[...]))
    y = x + gate * _dot(p_ref[...].astype(BF16), wp_ref[...])
    if final_norm:
        y = _rms(y, gf_ref[...])
    o_ref[...] = y


def _ple(x, p, g, wg, wp, gf, layer, final_norm):
    tm = PLE_ROW_TILE
    return pl.pallas_call(
        functools.partial(_ple_kernel, final_norm=final_norm),
        grid=(SEQ // tm,),
        in_specs=[
            pl.BlockSpec((tm, D_MODEL), lambda i: (i, 0)),
            pl.BlockSpec((None, None, tm, PLE_DIM), lambda i: (layer, 0, i, 0)),
            _const_spec(g.shape),
            pl.BlockSpec((None, D_MODEL, D_MODEL), lambda i: (layer, 0, 0), pipeline_mode=pl.Buffered(1)),
            pl.BlockSpec((None, PLE_DIM, D_MODEL), lambda i: (layer, 0, 0), pipeline_mode=pl.Buffered(1)),
            _const_spec(gf.shape),
        ],
        out_specs=pl.BlockSpec((tm, D_MODEL), lambda i: (i, 0)),
        out_shape=jax.ShapeDtypeStruct((SEQ, D_MODEL), F32),
        compiler_params=_params("arbitrary"),
        name="ple",
    )(x, p, g, wg, wp, gf)


def _row(v):
    return v.reshape(1, -1)


def kernel(x, p, positions, norm_mix, norm_ffn, norm_ple, w_in0, q_norm, kv_norm, w_qb, w_kvb, v_ln_g,
           v_ln_b, w_spatial, b_spatial, w_out0, w_in1, conv_w, w_out1, w_gate, w_up, w_down, w_ple_gate,
           w_ple_proj, norm_final):
    assert x.shape == (1, SEQ, D_MODEL) and p.shape == (2, 1, SEQ, PLE_DIM)
    xs = x.reshape(SEQ, D_MODEL)
    pos = positions.reshape(SEQ, 1)

    o1, o2, o3 = Q_LORA, Q_LORA + KV_LORA, Q_LORA + KV_LORA + QK_ROPE
    w0 = w_in0[0]
    k1, k2 = w0[:, o2:o2 + ROPE_HALF], w0[:, o2 + ROPE_HALF:o3]
    win_mla = jnp.concatenate([w0[:, :o2], k1, k2, k2, k1], axis=1).astype(BF16)
    wu0 = w0[:, o3:o3 + WIDTH].astype(BF16)
    wv0 = w0[:, o3 + WIDTH:].astype(BF16)
    wq = w_qb[0].reshape(Q_LORA, HEADS, QK_HEAD)
    pe = wq[:, :, HEAD_DIM:]
    pe_swapped = jnp.concatenate([pe[:, :, ROPE_HALF:], pe[:, :, :ROPE_HALF]], axis=-1)
    wqb = jnp.concatenate([wq[:, :, :HEAD_DIM].reshape(Q_LORA, WIDTH), pe.reshape(Q_LORA, HEADS * QK_ROPE),
                           pe_swapped.reshape(Q_LORA, HEADS * QK_ROPE)], axis=1).astype(BF16)
    wkv = w_kvb[0].reshape(KV_LORA, HEADS, 2 * HEAD_DIM)
    wkt = wkv[:, :, :HEAD_DIM].reshape(KV_LORA, WIDTH).T.astype(BF16)
    wv = wkv[:, :, HEAD_DIM:].reshape(KV_LORA, WIDTH).astype(BF16)
    inv_freq = ROPE_BASE ** (-jnp.arange(ROPE_HALF, dtype=F32) / ROPE_HALF)
    inv4 = jnp.tile(inv_freq, LANES // ROPE_HALF).reshape(1, LANES)
    sgn4 = jnp.tile(jnp.concatenate([-jnp.ones(ROPE_HALF, F32), jnp.ones(ROPE_HALF, F32)]), 2).reshape(1, LANES)
    cw = jnp.concatenate([conv_w[0], jnp.zeros((8 - CONV_W, D_MODEL), F32)], axis=0)
    wp_ple = w_ple_proj.astype(BF16)

    n_attn, n_row = SEQ // ATTN_TILE, SEQ // ROW_TILE

    def slab_cast(w, layer, steps):
        rows, cols = w.shape[1] // steps, w.shape[2]
        return _Cast(w, pl.BlockSpec((None, rows, cols), lambda i: (layer, i, 0)),
                     pl.BlockSpec((rows, cols), lambda i: (i, 0)),
                     jax.ShapeDtypeStruct(w.shape[1:], BF16))

    def tile_cast(w, layer):
        blk = (w.shape[1] // n_row, w.shape[2] // (D_MODEL // FF_TILE))
        return _Cast(w, pl.BlockSpec((None,) + blk, lambda i, j: (layer, i, j)),
                     pl.BlockSpec(blk, lambda i, j: (i, j)), jax.ShapeDtypeStruct(w.shape[1:], BF16))

    ple_rows = D_MODEL // n_row
    ple_gate_cast = _Cast(w_ple_gate, pl.BlockSpec((2, ple_rows, D_MODEL), lambda i: (0, i, 0)),
                          pl.BlockSpec((2, ple_rows, D_MODEL), lambda i: (0, i, 0)),
                          jax.ShapeDtypeStruct(w_ple_gate.shape, BF16))

    (q, kt, v), (wg0, wu0_ffn) = _mla_pre(
        xs, pos, _row(norm_mix[0]), win_mla, _row(q_norm[0]), _row(kv_norm[0]), wqb, wkt, wv, inv4, sgn4,
        casts=[slab_cast(w_gate, 0, n_attn), slab_cast(w_up, 0, n_attn)])
    attn = _flash(q, kt, v)
    (gm,), (wd0, win1, wo0) = _gmlp(
        xs, _row(norm_mix[0]), wu0, wv0, _row(v_ln_g[0]), _row(v_ln_b[0]), w_spatial[0], b_spatial[0].T,
        casts=[slab_cast(w_down, 0, n_row), slab_cast(w_in1, 0, n_row), slab_cast(w_out0, 0, n_row)])
    (xs,), (wo1, wg_ple) = _outproj(xs, attn, gm, wo0, casts=[slab_cast(w_out1, 0, n_row), ple_gate_cast])
    xs = _ffn(xs, _row(norm_ffn[0]), wg0, wu0_ffn, wd0)
    xs = _ple(xs, p, _row(norm_ple[0]), wg_ple, wp_ple, _row(norm_final), layer=0, final_norm=False)

    (xs,), (wg1, wu1, wd1) = _conv_mixer(
        xs, _row(norm_mix[1]), win1, cw, wo1,
        casts=[tile_cast(w_gate, 1), tile_cast(w_up, 1), tile_cast(w_down, 1)])
    xs = _ffn(xs, _row(norm_ffn[1]), wg1, wu1, wd1)
    xs = _ple(xs, p, _row(norm_ple[1]), wg_ple, wp_ple, _row(norm_final), layer=1, final_norm=True)
    return xs.reshape(1, SEQ, D_MODEL)
```

```python
import functools
import math
from typing import NamedTuple

import jax
import jax.numpy as jnp
from jax import lax
from jax.experimental import pallas as pl
from jax.experimental.pallas import tpu as pltpu

F32 = jnp.float32
BF16 = jnp.bfloat16

D_MODEL = 2048
SEQ = 16384
EPS = 1e-6
PLE_DIM = 256
HEADS = 8
HEAD_DIM = 128
CHUNK = 128
Q_LORA = 512
KV_LORA = 256
QK_ROPE = 64
ROPE_HALF = QK_ROPE // 2
QK_HEAD = HEAD_DIM + QK_ROPE
WIDTH = HEADS * HEAD_DIM
ROPE_WIDTH = HEADS * QK_ROPE
ROPE_BASE = 10000.0
D_FF = 5632
CONV_W = 3

VMEM_LIMIT_BYTES = 56 * 1024 * 1024
LANES = 128

ROW_TILE = 512
FFN_ROW_TILE = 1024
ATTN_TILE = 512
Q_TILE = 1024
HEAD_GROUP = 4
PLE_ROW_TILE = 1024
FF_TILE = 512
Q_SCALE = (QK_HEAD ** -0.5) * math.log2(math.e)
MASK_VALUE = -1e30


def _params(*semantics):
    return pltpu.CompilerParams(dimension_semantics=semantics, vmem_limit_bytes=VMEM_LIMIT_BYTES)


def _const_spec(shape):
    zeros = (0,) * len(shape)
    return pl.BlockSpec(shape, lambda *_: zeros, pipeline_mode=pl.Buffered(1))


class _Cast(NamedTuple):
    src: jax.Array
    in_spec: pl.BlockSpec
    out_spec: pl.BlockSpec
    out_shape: jax.ShapeDtypeStruct


def _with_casts(body, n_in, n_out, n_cast):
    def kern(*refs):
        ins = refs[:n_in]
        cast_ins = refs[n_in:n_in + n_cast]
        outs = refs[n_in + n_cast:n_in + n_cast + n_out]
        cast_outs = refs[n_in + n_cast + n_out:n_in + 2 * n_cast + n_out]
        scratch = refs[n_in + 2 * n_cast + n_out:]
        for s, d in zip(cast_ins, cast_outs):
            d[...] = s[...].astype(BF16)
        body(*ins, *outs, *scratch)
    return kern


def _call_with_casts(body, inputs, in_specs, out_specs, out_shape, casts, **kwargs):
    casts = tuple(casts)
    n_out = len(out_specs)
    res = pl.pallas_call(
        _with_casts(body, len(inputs), n_out, len(casts)),
        in_specs=list(in_specs) + [c.in_spec for c in casts],
        out_specs=list(out_specs) + [c.out_spec for c in casts],
        out_shape=list(out_shape) + [c.out_shape for c in casts],
        **kwargs,
    )(*inputs, *[c.src for c in casts])
    return res[:n_out], res[n_out:]


def _rms(x, g):
    return x * lax.rsqrt(jnp.mean(x * x, axis=-1, keepdims=True) + EPS) * g


def _dot(a, b):
    return jnp.dot(a, b, preferred_element_type=F32)


def _dot_nt(a, b):
    return lax.dot_general(a, b, (((1,), (1,)), ((), ())), preferred_element_type=F32)


def _mla_pre_kernel(x_ref, pos_ref, g_ref, win_ref, qn_ref, kvn_ref, wqb_ref, wkt_ref, wv_ref,
                    inv_ref, sgn_ref, q_ref, kt_ref, v_ref):
    h = _rms(x_ref[...], g_ref[...]).astype(BF16)
    z = _dot(h, win_ref[...])
    q_lat = z[:, :Q_LORA]
    kv_lat = z[:, Q_LORA:Q_LORA + KV_LORA]
    kk = z[:, Q_LORA + KV_LORA:]

    ang = pos_ref[...].astype(F32) * inv_ref[...]
    cos = jnp.cos(ang)
    sin = jnp.sin(ang) * sgn_ref[...]

    qn = _rms(q_lat, qn_ref[...]).astype(BF16)
    qf = _dot(qn, wqb_ref[...])
    cos4 = jnp.concatenate([cos] * 4, axis=-1)
    sin4 = jnp.concatenate([sin] * 4, axis=-1)
    q_pe = qf[:, WIDTH:WIDTH + ROPE_WIDTH] * cos4 + qf[:, WIDTH + ROPE_WIDTH:] * sin4
    for hd in range(HEADS):
        q_ref[hd, :, 0:HEAD_DIM] = (qf[:, hd * HEAD_DIM:(hd + 1) * HEAD_DIM] * Q_SCALE).astype(BF16)
        q_ref[hd, :, HEAD_DIM:QK_HEAD] = (q_pe[:, hd * QK_ROPE:(hd + 1) * QK_ROPE] * Q_SCALE).astype(BF16)

    lane = lax.broadcasted_iota(jnp.int32, ang.shape, 1)
    t = kk * jnp.where(lane < QK_ROPE, cos, sin)
    k_pe = t + pltpu.roll(t, QK_ROPE, axis=1)
    k_pe_t = k_pe.T[:QK_ROPE, :].astype(BF16)

    kvn = _rms(kv_lat, kvn_ref[...]).astype(BF16)
    kn_t = _dot_nt(wkt_ref[...], kvn)
    vv = _dot(kvn, wv_ref[...])
    for hd in range(HEADS):
        kt_ref[hd, 0, 0:HEAD_DIM, :] = kn_t[hd * HEAD_DIM:(hd + 1) * HEAD_DIM, :].astype(BF16)
        kt_ref[hd, 0, HEAD_DIM:QK_HEAD, :] = k_pe_t
        v_ref[hd] = vv[:, hd * HEAD_DIM:(hd + 1) * HEAD_DIM].astype(BF16)


def _mla_pre(x, pos, g, win, qn, kvn, wqb, wkt, wv, inv4, sgn4, casts=()):
    tm = ATTN_TILE
    n = SEQ // tm
    return _call_with_casts(
        _mla_pre_kernel,
        (x, pos, g, win, qn, kvn, wqb, wkt, wv, inv4, sgn4),
        grid=(n,),
        in_specs=[
            pl.BlockSpec((tm, D_MODEL), lambda i: (i, 0)),
            pl.BlockSpec((tm, 1), lambda i: (i, 0)),
            _const_spec(g.shape), _const_spec(win.shape), _const_spec(qn.shape), _const_spec(kvn.shape),
            _const_spec(wqb.shape), _const_spec(wkt.shape), _const_spec(wv.shape),
            _const_spec(inv4.shape), _const_spec(sgn4.shape),
        ],
        out_specs=[
            pl.BlockSpec((HEADS, tm, QK_HEAD), lambda i: (0, i, 0)),
            pl.BlockSpec((HEADS, 1, QK_HEAD, tm), lambda i: (0, i, 0, 0)),
            pl.BlockSpec((HEADS, tm, HEAD_DIM), lambda i: (0, i, 0)),
        ],
        out_shape=[
            jax.ShapeDtypeStruct((HEADS, SEQ, QK_HEAD), BF16),
            jax.ShapeDtypeStruct((HEADS, n, QK_HEAD, tm), BF16),
            jax.ShapeDtypeStruct((HEADS, SEQ, HEAD_DIM), BF16),
        ],
        casts=casts,
        compiler_params=_params("arbitrary"),
        name="mla_pre",
    )


def _flash_kernel(q_ref, kt_ref, v_ref, o_ref, m_sc, l_sc, acc_sc):
    qi = pl.program_id(1)
    tq, tk = Q_TILE, ATTN_TILE
    ratio = tq // tk
    ncol = tk // LANES
    m_sc[...] = jnp.full(m_sc.shape, MASK_VALUE, F32)
    l_sc[...] = jnp.zeros(l_sc.shape, F32)
    acc_sc[...] = jnp.zeros(acc_sc.shape, F32)

    def step(j, diag):
        start = pl.multiple_of(j * tk, tk)
        rows = slice(None) if diag is None else slice(diag * tk, tq)
        for g in range(HEAD_GROUP):
            s = _dot(q_ref[g, rows, :], kt_ref[g, j])
            if diag is not None:
                row = lax.broadcasted_iota(jnp.int32, s.shape, 0)
                col = lax.broadcasted_iota(jnp.int32, s.shape, 1)
                s = jnp.where(col <= row, s, MASK_VALUE)
            cols = [s[:, c * LANES:(c + 1) * LANES] for c in range(ncol)]
            m_prev = m_sc[g, rows, :]
            m_cur = functools.reduce(jnp.maximum, cols)
            m_new = jnp.maximum(m_prev, jnp.max(m_cur, axis=-1, keepdims=True))
            alpha = jnp.exp2(m_prev - m_new)
            ps = [jnp.exp2((c - m_new).astype(BF16)) for c in cols]
            l_sc[g, rows, :] = alpha * l_sc[g, rows, :] + functools.reduce(jnp.add, ps).astype(F32)
            p = jnp.concatenate(ps, axis=-1)
            acc_sc[g, rows, :] = alpha * acc_sc[g, rows, :] + _dot(p, v_ref[g, pl.ds(start, tk), :])
            m_sc[g, rows, :] = m_new

    def body(j, carry):
        step(j, None)
        return carry

    lax.fori_loop(0, qi * ratio, body, 0)
    for d in range(ratio):
        step(qi * ratio + d, d)
    for g in range(HEAD_GROUP):
        l = jnp.sum(l_sc[g], axis=-1, keepdims=True)
        o_ref[:, g * HEAD_DIM:(g + 1) * HEAD_DIM] = (acc_sc[g] / l).astype(BF16)


def _flash(q, kt, v):
    tq, tk = Q_TILE, ATTN_TILE
    hg = HEAD_GROUP
    return pl.pallas_call(
        _flash_kernel,
        grid=(HEADS // hg, SEQ // tq),
        in_specs=[
            pl.BlockSpec((hg, tq, QK_HEAD), lambda h, i: (h, i, 0)),
            pl.BlockSpec((hg, SEQ // tk, QK_HEAD, tk), lambda h, i: (h, 0, 0, 0), pipeline_mode=pl.Buffered(1)),
            pl.BlockSpec((hg, SEQ, HEAD_DIM), lambda h, i: (h, 0, 0), pipeline_mode=pl.Buffered(1)),
        ],
        out_specs=pl.BlockSpec((tq, hg * HEAD_DIM), lambda h, i: (i, h)),
        out_shape=jax.ShapeDtypeStruct((SEQ, WIDTH), BF16),
        scratch_shapes=[
            pltpu.VMEM((hg, tq, LANES), F32),
            pltpu.VMEM((hg, tq, LANES), F32),
            pltpu.VMEM((hg, tq, HEAD_DIM), F32),
        ],
        compiler_params=_params("arbitrary", "arbitrary"),
        name="mla_flash",
    )(q, kt, v)


def _gelu(x):
    return jax.nn.gelu(x)


def _gmlp_kernel(x_ref, g_ref, wu_ref, wv_ref, lng_ref, lnb_ref, ws_ref, bst_ref, o_ref):
    tm = x_ref.shape[0]
    h = _rms(x_ref[...], g_ref[...]).astype(BF16)
    u = _gelu(_dot(h, wu_ref[...]))
    gv = _gelu(_dot(h, wv_ref[...]))
    mu = jnp.mean(gv, axis=-1, keepdims=True)
    var = jnp.mean(jnp.square(gv - mu), axis=-1, keepdims=True)
    vn = ((gv - mu) * lax.rsqrt(var + EPS) * lng_ref[...] + lnb_ref[...]).astype(BF16)
    row = lax.broadcasted_iota(jnp.int32, (CHUNK, CHUNK), 0)
    col = lax.broadcasted_iota(jnp.int32, (CHUNK, CHUNK), 1)
    bst = bst_ref[...]
    for hd in range(HEADS):
        w = jnp.where(col <= row, ws_ref[hd], 0.0).astype(BF16)
        bias = bst[:, hd:hd + 1]
        cs = slice(hd * HEAD_DIM, (hd + 1) * HEAD_DIM)
        for c in range(tm // CHUNK):
            rs = slice(c * CHUNK, (c + 1) * CHUNK)
            gate = _dot(w, vn[rs, cs]) + bias
            o_ref[rs, cs] = (u[rs, cs] * gate).astype(BF16)


def _gmlp(x, g, wu, wv, lng, lnb, ws, bst, casts=()):
    tm = ROW_TILE
    return _call_with_casts(
        _gmlp_kernel,
        (x, g, wu, wv, lng, lnb, ws, bst),
        grid=(SEQ // tm,),
        in_specs=[
            pl.BlockSpec((tm, D_MODEL), lambda i: (i, 0)),
            _const_spec(g.shape), _const_spec(wu.shape), _const_spec(wv.shape),
            _const_spec(lng.shape), _const_spec(lnb.shape), _const_spec(ws.shape), _const_spec(bst.shape),
        ],
        out_specs=[pl.BlockSpec((tm, WIDTH), lambda i: (i, 0))],
        out_shape=[jax.ShapeDtypeStruct((SEQ, WIDTH), BF16)],
        casts=casts,
        compiler_params=_params("arbitrary"),
        name="gmlp",
    )


def _outproj_kernel(x_ref, a_ref, b_ref, wa_ref, wb_ref, o_ref):
    o_ref[...] = x_ref[...] + _dot(a_ref[...], wa_ref[...]) + _dot(b_ref[...], wb_ref[...])


def _outproj(x, a, b, w, casts=()):
    tm = ROW_TILE
    return _call_with_casts(
        _outproj_kernel,
        (x, a, b, w, w),
        grid=(SEQ // tm,),
        in_specs=[
            pl.BlockSpec((tm, D_MODEL), lambda i: (i, 0)),
            pl.BlockSpec((tm, WIDTH), lambda i: (i, 0)),
            pl.BlockSpec((tm, WIDTH), lambda i: (i, 0)),
            pl.BlockSpec((WIDTH, D_MODEL), lambda i: (0, 0), pipeline_mode=pl.Buffered(1)),
            pl.BlockSpec((WIDTH, D_MODEL), lambda i: (1, 0), pipeline_mode=pl.Buffered(1)),
        ],
        out_specs=[pl.BlockSpec((tm, D_MODEL), lambda i: (i, 0))],
        out_shape=[jax.ShapeDtypeStruct((SEQ, D_MODEL), F32)],
        casts=casts,
        compiler_params=_params("arbitrary"),
        name="outproj0",
    )


def _ffn_kernel(x_ref, g_ref, wg_ref, wu_ref, wd_ref, o_ref, h_sc):
    @pl.when(pl.program_id(1) == 0)
    def _():
        x = x_ref[...]
        h_sc[...] = _rms(x, g_ref[...]).astype(BF16)
        o_ref[...] = x

    h = h_sc[...]
    half = wg_ref.shape[1] // 2
    acc = None
    for s in range(2):
        cs = slice(s * half, (s + 1) * half)
        a = _dot(h, wg_ref[:, cs])
        b = _dot(h, wu_ref[:, cs])
        t = (a * jax.nn.sigmoid(a) * b).astype(BF16)
        part = _dot(t, wd_ref[cs, :])
        acc = part if acc is None else acc + part
    o_ref[...] += acc


def _ffn(x, g, wg, wu, wd):
    tm, tf = FFN_ROW_TILE, FF_TILE
    return pl.pallas_call(
        _ffn_kernel,
        grid=(SEQ // tm, D_FF // tf),
        in_specs=[
            pl.BlockSpec((tm, D_MODEL), lambda i, j: (i, 0)),
            _const_spec(g.shape),
            pl.BlockSpec((D_MODEL, tf), lambda i, j: (0, j)),
            pl.BlockSpec((D_MODEL, tf), lambda i, j: (0, j)),
            pl.BlockSpec((tf, D_MODEL), lambda i, j: (j, 0)),
        ],
        out_specs=pl.BlockSpec((tm, D_MODEL), lambda i, j: (i, 0)),
        out_shape=jax.ShapeDtypeStruct((SEQ, D_MODEL), F32),
        scratch_shapes=[pltpu.VMEM((tm, D_MODEL), BF16)],
        compiler_params=_params("arbitrary", "arbitrary"),
        name="ffn",
    )(x, g, wg, wu, wd)


def _conv_kernel(x_ref, g_ref, wb_ref, wc_ref, wh_ref, cw_ref, wo_ref, p_ref, gp_ref, wgp_ref, wpp_ref,
                 o_ref, h_sc, tail_sc):
    i = pl.program_id(0)
    j = pl.program_id(1)

    @pl.when(j == 0)
    def _():
        x = x_ref[...]
        hp = _rms(x, gp_ref[...]).astype(BF16)
        gate = jax.nn.sigmoid(_dot(hp, wgp_ref[...]))
        x = x + gate * _dot(p_ref[...].astype(BF16), wpp_ref[...])
        h_sc[...] = _rms(x, g_ref[...]).astype(BF16)
        o_ref[...] = x

    @pl.when(i == 0)
    def _():
        tail_sc[j] = jnp.zeros(tail_sc.shape[1:], F32)

    h = h_sc[...]
    tail = tail_sc[j]
    cw = cw_ref[...]
    half = wb_ref.shape[1] // 2
    acc = None
    for s in range(2):
        cs = slice(s * half, (s + 1) * half)
        gb = _dot(h, wb_ref[:, cs])
        cz = _dot(h, wc_ref[:, cs]) * _dot(h, wh_ref[:, cs])
        prev1 = tail[7:8, cs]
        prev2 = tail[6:7, cs]
        row = lax.broadcasted_iota(jnp.int32, cz.shape, 0)
        z1 = jnp.where(row == 0, prev1, pltpu.roll(cz, 1, axis=0))
        z2 = jnp.where(row == 0, prev2, jnp.where(row == 1, prev1, pltpu.roll(cz, 2, axis=0)))
        y = cw[0:1, cs] * z2 + cw[1:2, cs] * z1 + cw[2:3, cs] * cz
        tail_sc[j, :, cs] = cz[cz.shape[0] - 8:, :]
        part = _dot((gb * y).astype(BF16), wo_ref[cs, :])
        acc = part if acc is None else acc + part
    o_ref[...] += acc


def _conv_mixer(x, g, win, cw, wo, p, gp, wgp, wpp, casts=()):
    tm, tc = ROW_TILE, FF_TILE
    nc = D_MODEL // tc
    return _call_with_casts(
        _conv_kernel,
        (x, g, win, win, win, cw, wo, p, gp, wgp, wpp),
        grid=(SEQ // tm, nc),
        in_specs=[
            pl.BlockSpec((tm, D_MODEL), lambda i, j: (i, 0)),
            _const_spec(g.shape),
            pl.BlockSpec((D_MODEL, tc), lambda i, j: (0, j)),
            pl.BlockSpec((D_MODEL, tc), lambda i, j: (0, j + nc)),
            pl.BlockSpec((D_MODEL, tc), lambda i, j: (0, j + 2 * nc)),
            pl.BlockSpec((8, tc), lambda i, j: (0, j)),
            pl.BlockSpec((tc, D_MODEL), lambda i, j: (j, 0)),
            pl.BlockSpec((None, None, tm, PLE_DIM), lambda i, j: (0, 0, i, 0)),
            _const_spec(gp.shape),
            pl.BlockSpec((None, D_MODEL, D_MODEL), lambda i, j: (0, 0, 0), pipeline_mode=pl.Buffered(1)),
            pl.BlockSpec((None, PLE_DIM, D_MODEL), lambda i, j: (0, 0, 0), pipeline_mode=pl.Buffered(1)),
        ],
        out_specs=[pl.BlockSpec((tm, D_MODEL), lambda i, j: (i, 0))],
        out_shape=[jax.ShapeDtypeStruct((SEQ, D_MODEL), F32)],
        casts=casts,
        scratch_shapes=[pltpu.VMEM((tm, D_MODEL), BF16), pltpu.VMEM((nc, 8, tc), F32)],
        compiler_params=_params("arbitrary", "arbitrary"),
        name="conv_mixer",
    )


def _ple_kernel(x_ref, p_ref, g_ref, wg_ref, wp_ref, gf_ref, o_ref, *, final_norm):
    x = x_ref[...]
    h = _rms(x, g_ref[...]).astype(BF16)
    gate = jax.nn.sigmoid(_dot(h, wg_ref[...]))
    y = x + gate * _dot(p_ref[...].astype(BF16), wp_ref[...])
    if final_norm:
        y = _rms(y, gf_ref[...])
    o_ref[...] = y


def _ple(x, p, g, wg, wp, gf, layer, final_norm):
    tm = PLE_ROW_TILE
    return pl.pallas_call(
        functools.partial(_ple_kernel, final_norm=final_norm),
        grid=(SEQ // tm,),
        in_specs=[
            pl.BlockSpec((tm, D_MODEL), lambda i: (i, 0)),
            pl.BlockSpec((None, None, tm, PLE_DIM), lambda i: (layer, 0, i, 0)),
            _const_spec(g.shape),
            pl.BlockSpec((None, D_MODEL, D_MODEL), lambda i: (layer, 0, 0), pipeline_mode=pl.Buffered(1)),
            pl.BlockSpec((None, PLE_DIM, D_MODEL), lambda i: (layer, 0, 0), pipeline_mode=pl.Buffered(1)),
            _const_spec(gf.shape),
        ],
        out_specs=pl.BlockSpec((tm, D_MODEL), lambda i: (i, 0)),
        out_shape=jax.ShapeDtypeStruct((SEQ, D_MODEL), F32),
        compiler_params=_params("arbitrary"),
        name="ple",
    )(x, p, g, wg, wp, gf)


def _row(v):
    return v.reshape(1, -1)


def kernel(x, p, positions, norm_mix, norm_ffn, norm_ple, w_in0, q_norm, kv_norm, w_qb, w_kvb, v_ln_g,
           v_ln_b, w_spatial, b_spatial, w_out0, w_in1, conv_w, w_out1, w_gate, w_up, w_down, w_ple_gate,
           w_ple_proj, norm_final):
    assert x.shape == (1, SEQ, D_MODEL) and p.shape == (2, 1, SEQ, PLE_DIM)
    xs = x.reshape(SEQ, D_MODEL)
    pos = positions.reshape(SEQ, 1)

    o1, o2, o3 = Q_LORA, Q_LORA + KV_LORA, Q_LORA + KV_LORA + QK_ROPE
    w0 = w_in0[0]
    k1, k2 = w0[:, o2:o2 + ROPE_HALF], w0[:, o2 + ROPE_HALF:o3]
    win_mla = jnp.concatenate([w0[:, :o2], k1, k2, k2, k1], axis=1).astype(BF16)
    wu0 = w0[:, o3:o3 + WIDTH].astype(BF16)
    wv0 = w0[:, o3 + WIDTH:].astype(BF16)
    wq = w_qb[0].reshape(Q_LORA, HEADS, QK_HEAD)
    pe = wq[:, :, HEAD_DIM:]
    pe_swapped = jnp.concatenate([pe[:, :, ROPE_HALF:], pe[:, :, :ROPE_HALF]], axis=-1)
    wqb = jnp.concatenate([wq[:, :, :HEAD_DIM].reshape(Q_LORA, WIDTH), pe.reshape(Q_LORA, HEADS * QK_ROPE),
                           pe_swapped.reshape(Q_LORA, HEADS * QK_ROPE)], axis=1).astype(BF16)
    wkv = w_kvb[0].reshape(KV_LORA, HEADS, 2 * HEAD_DIM)
    wkt = wkv[:, :, :HEAD_DIM].reshape(KV_LORA, WIDTH).T.astype(BF16)
    wv = wkv[:, :, HEAD_DIM:].reshape(KV_LORA, WIDTH).astype(BF16)
    inv_freq = ROPE_BASE ** (-jnp.arange(ROPE_HALF, dtype=F32) / ROPE_HALF)
    inv4 = jnp.tile(inv_freq, LANES // ROPE_HALF).reshape(1, LANES)
    sgn4 = jnp.tile(jnp.concatenate([-jnp.ones(ROPE_HALF, F32), jnp.ones(ROPE_HALF, F32)]), 2).reshape(1, LANES)
    cw = jnp.concatenate([conv_w[0], jnp.zeros((8 - CONV_W, D_MODEL), F32)], axis=0)
    wp_ple = w_ple_proj.astype(BF16)

    n_attn, n_row = SEQ // ATTN_TILE, SEQ // ROW_TILE

    def slab_cast(w, layer, steps):
        rows, cols = w.shape[1] // steps, w.shape[2]
        return _Cast(w, pl.BlockSpec((None, rows, cols), lambda i: (layer, i, 0)),
                     pl.BlockSpec((rows, cols), lambda i: (i, 0)),
                     jax.ShapeDtypeStruct(w.shape[1:], BF16))

    def tile_cast(w, layer):
        blk = (w.shape[1] // n_row, w.shape[2] // (D_MODEL // FF_TILE))
        return _Cast(w, pl.BlockSpec((None,) + blk, lambda i, j: (layer, i, j)),
                     pl.BlockSpec(blk, lambda i, j: (i, j)), jax.ShapeDtypeStruct(w.shape[1:], BF16))

    ple_rows = D_MODEL // n_row
    ple_gate_cast = _Cast(w_ple_gate, pl.BlockSpec((2, ple_rows, D_MODEL), lambda i: (0, i, 0)),
                          pl.BlockSpec((2, ple_rows, D_MODEL), lambda i: (0, i, 0)),
                          jax.ShapeDtypeStruct(w_ple_gate.shape, BF16))

    (q, kt, v), (wg0, wu0_ffn) = _mla_pre(
        xs, pos, _row(norm_mix[0]), win_mla, _row(q_norm[0]), _row(kv_norm[0]), wqb, wkt, wv, inv4, sgn4,
        casts=[slab_cast(w_gate, 0, n_attn), slab_cast(w_up, 0, n_attn)])
    attn = _flash(q, kt, v)
    (gm,), (wd0, win1, wo0) = _gmlp(
        xs, _row(norm_mix[0]), wu0, wv0, _row(v_ln_g[0]), _row(v_ln_b[0]), w_spatial[0], b_spatial[0].T,
        casts=[slab_cast(w_down, 0, n_row), slab_cast(w_in1, 0, n_row), slab_cast(w_out0, 0, n_row)])
    (xs,), (wo1, wg_ple) = _outproj(xs, attn, gm, wo0, casts=[slab_cast(w_out1, 0, n_row), ple_gate_cast])
    xs = _ffn(xs, _row(norm_ffn[0]), wg0, wu0_ffn, wd0)

    (xs,), (wg1, wu1, wd1) = _conv_mixer(
        xs, _row(norm_mix[1]), win1, cw, wo1, p, _row(norm_ple[0]), wg_ple, wp_ple,
        casts=[tile_cast(w_gate, 1), tile_cast(w_up, 1), tile_cast(w_down, 1)])
    xs = _ffn(xs, _row(norm_ffn[1]), wg1, wu1, wd1)
    xs = _ple(xs, p, _row(norm_ple[1]), wg_ple, wp_ple, _row(norm_final), layer=1, final_norm=True)
    return xs.reshape(1, SEQ, D_MODEL)
```

```python
import functools
import math
from typing import NamedTuple

import jax
import jax.numpy as jnp
from jax import lax
from jax.experimental import pallas as pl
from jax.experimental.pallas import tpu as pltpu

F32 = jnp.float32
BF16 = jnp.bfloat16

D_MODEL = 2048
SEQ = 16384
EPS = 1e-6
PLE_DIM = 256
HEADS = 8
HEAD_DIM = 128
CHUNK = 128
Q_LORA = 512
KV_LORA = 256
QK_ROPE = 64
ROPE_HALF = QK_ROPE // 2
QK_HEAD = HEAD_DIM + QK_ROPE
WIDTH = HEADS * HEAD_DIM
ROPE_WIDTH = HEADS * QK_ROPE
ROPE_BASE = 10000.0
D_FF = 5632
CONV_W = 3

VMEM_LIMIT_BYTES = 56 * 1024 * 1024
LANES = 128

ROW_TILE = 512
FFN_ROW_TILE = 1024
ATTN_TILE = 512
Q_TILE = 1024
HEAD_GROUP = 4
PLE_ROW_TILE = 1024
FF_TILE = 512
Q_SCALE = (QK_HEAD ** -0.5) * math.log2(math.e)
MASK_VALUE = -1e30


def _params(*semantics):
    return pltpu.CompilerParams(dimension_semantics=semantics, vmem_limit_bytes=VMEM_LIMIT_BYTES)


def _const_spec(shape):
    zeros = (0,) * len(shape)
    return pl.BlockSpec(shape, lambda *_: zeros, pipeline_mode=pl.Buffered(1))


class _Cast(NamedTuple):
    src: jax.Array
    in_spec: pl.BlockSpec
    out_spec: pl.BlockSpec
    out_shape: jax.ShapeDtypeStruct


RING_SLOTS = 3


def _with_casts(body, n_in, n_out, n_cast, ring_rows=None):
    def kern(*refs):
        ins = list(refs[:n_in])
        cast_ins = refs[n_in:n_in + n_cast]
        outs = refs[n_in + n_cast:n_in + n_cast + n_out]
        cast_outs = refs[n_in + n_cast + n_out:n_in + 2 * n_cast + n_out]
        scratch = list(refs[n_in + 2 * n_cast + n_out:])
        if ring_rows is not None:
            sem = scratch.pop()
            buf = scratch.pop()
            src = ins[0]
            i = pl.program_id(0)
            n = pl.num_programs(0)

            def tile_copy(step):
                slot = step % RING_SLOTS
                rows = pl.ds(pl.multiple_of(step * ring_rows, ring_rows), ring_rows)
                return pltpu.make_async_copy(src.at[rows, :], buf.at[slot], sem.at[slot])

            @pl.when(i == 0)
            def _():
                for s in range(RING_SLOTS - 1):
                    tile_copy(s).start()

            @pl.when(i + (RING_SLOTS - 1) < n)
            def _():
                tile_copy(i + (RING_SLOTS - 1)).start()

            tile_copy(i).wait()
            ins[0] = buf.at[i % RING_SLOTS]
        for s, d in zip(cast_ins, cast_outs):
            d[...] = s[...].astype(BF16)
        body(*ins, *outs, *scratch)
    return kern


def _call_with_casts(body, inputs, in_specs, out_specs, out_shape, casts, ring_rows=None, scratch_shapes=(), **kwargs):
    casts = tuple(casts)
    n_out = len(out_specs)
    in_specs = list(in_specs)
    scratch_shapes = list(scratch_shapes)
    if ring_rows is not None:
        assert len(kwargs["grid"]) == 1 and kwargs["grid"][0] >= RING_SLOTS - 1
        in_specs[0] = pl.BlockSpec(memory_space=pl.ANY)
        scratch_shapes += [pltpu.VMEM((RING_SLOTS, ring_rows) + inputs[0].shape[1:], inputs[0].dtype),
                           pltpu.SemaphoreType.DMA((RING_SLOTS,))]
    res = pl.pallas_call(
        _with_casts(body, len(inputs), n_out, len(casts), ring_rows),
        in_specs=in_specs + [c.in_spec for c in casts],
        out_specs=list(out_specs) + [c.out_spec for c in casts],
        out_shape=list(out_shape) + [c.out_shape for c in casts],
        scratch_shapes=scratch_shapes,
        **kwargs,
    )(*inputs, *[c.src for c in casts])
    return res[:n_out], res[n_out:]


def _rms(x, g):
    return x * lax.rsqrt(jnp.mean(x * x, axis=-1, keepdims=True) + EPS) * g


def _dot(a, b):
    return jnp.dot(a, b, preferred_element_type=F32)


def _dot_nt(a, b):
    return lax.dot_general(a, b, (((1,), (1,)), ((), ())), preferred_element_type=F32)


def _mla_pre_kernel(x_ref, pos_ref, g_ref, win_ref, qn_ref, kvn_ref, wqb_ref, wkt_ref, wv_ref,
                    inv_ref, sgn_ref, q_ref, kt_ref, v_ref):
    h = _rms(x_ref[...], g_ref[...]).astype(BF16)
    z = _dot(h, win_ref[...])
    q_lat = z[:, :Q_LORA]
    kv_lat = z[:, Q_LORA:Q_LORA + KV_LORA]
    kk = z[:, Q_LORA + KV_LORA:]

    ang = pos_ref[...].astype(F32) * inv_ref[...]
    cos = jnp.cos(ang)
    sin = jnp.sin(ang) * sgn_ref[...]

    qn = _rms(q_lat, qn_ref[...]).astype(BF16)
    qf = _dot(qn, wqb_ref[...])
    cos4 = jnp.concatenate([cos] * 4, axis=-1)
    sin4 = jnp.concatenate([sin] * 4, axis=-1)
    q_pe = qf[:, WIDTH:WIDTH + ROPE_WIDTH] * cos4 + qf[:, WIDTH + ROPE_WIDTH:] * sin4
    for hd in range(HEADS):
        q_ref[hd, :, 0:HEAD_DIM] = (qf[:, hd * HEAD_DIM:(hd + 1) * HEAD_DIM] * Q_SCALE).astype(BF16)
        q_ref[hd, :, HEAD_DIM:QK_HEAD] = (q_pe[:, hd * QK_ROPE:(hd + 1) * QK_ROPE] * Q_SCALE).astype(BF16)

    lane = lax.broadcasted_iota(jnp.int32, ang.shape, 1)
    t = kk * jnp.where(lane < QK_ROPE, cos, sin)
    k_pe = t + pltpu.roll(t, QK_ROPE, axis=1)
    k_pe_t = k_pe.T[:QK_ROPE, :].astype(BF16)

    kvn = _rms(kv_lat, kvn_ref[...]).astype(BF16)
    kn_t = _dot_nt(wkt_ref[...], kvn)
    vv = _dot(kvn, wv_ref[...])
    for hd in range(HEADS):
        kt_ref[hd, 0, 0:HEAD_DIM, :] = kn_t[hd * HEAD_DIM:(hd + 1) * HEAD_DIM, :].astype(BF16)
        kt_ref[hd, 0, HEAD_DIM:QK_HEAD, :] = k_pe_t
        v_ref[hd] = vv[:, hd * HEAD_DIM:(hd + 1) * HEAD_DIM].astype(BF16)


def _mla_pre(x, pos, g, win, qn, kvn, wqb, wkt, wv, inv4, sgn4, casts=()):
    tm = ATTN_TILE
    n = SEQ // tm
    return _call_with_casts(
        _mla_pre_kernel,
        (x, pos, g, win, qn, kvn, wqb, wkt, wv, inv4, sgn4),
        grid=(n,),
        in_specs=[
            pl.BlockSpec((tm, D_MODEL), lambda i: (i, 0)),
            pl.BlockSpec((tm, 1), lambda i: (i, 0)),
            _const_spec(g.shape), _const_spec(win.shape), _const_spec(qn.shape), _const_spec(kvn.shape),
            _const_spec(wqb.shape), _const_spec(wkt.shape), _const_spec(wv.shape),
            _const_spec(inv4.shape), _const_spec(sgn4.shape),
        ],
        out_specs=[
            pl.BlockSpec((HEADS, tm, QK_HEAD), lambda i: (0, i, 0)),
            pl.BlockSpec((HEADS, 1, QK_HEAD, tm), lambda i: (0, i, 0, 0)),
            pl.BlockSpec((HEADS, tm, HEAD_DIM), lambda i: (0, i, 0)),
        ],
        out_shape=[
            jax.ShapeDtypeStruct((HEADS, SEQ, QK_HEAD), BF16),
            jax.ShapeDtypeStruct((HEADS, n, QK_HEAD, tm), BF16),
            jax.ShapeDtypeStruct((HEADS, SEQ, HEAD_DIM), BF16),
        ],
        casts=casts,
        ring_rows=tm,
        compiler_params=_params("arbitrary"),
        name="mla_pre",
    )


def _flash_kernel(q_ref, kt_ref, v_ref, o_ref, m_sc, l_sc, acc_sc):
    qi = pl.program_id(1)
    tq, tk = Q_TILE, ATTN_TILE
    ratio = tq // tk
    ncol = tk // LANES
    m_sc[...] = jnp.full(m_sc.shape, MASK_VALUE, F32)
    l_sc[...] = jnp.zeros(l_sc.shape, F32)
    acc_sc[...] = jnp.zeros(acc_sc.shape, F32)

    def step(j, diag):
        start = pl.multiple_of(j * tk, tk)
        rows = slice(None) if diag is None else slice(diag * tk, tq)
        for g in range(HEAD_GROUP):
            s = _dot(q_ref[g, rows, :], kt_ref[g, j])
            if diag is not None:
                row = lax.broadcasted_iota(jnp.int32, s.shape, 0)
                col = lax.broadcasted_iota(jnp.int32, s.shape, 1)
                s = jnp.where(col <= row, s, MASK_VALUE)
            cols = [s[:, c * LANES:(c + 1) * LANES] for c in range(ncol)]
            m_prev = m_sc[g, rows, :]
            m_cur = functools.reduce(jnp.maximum, cols)
            m_new = jnp.maximum(m_prev, jnp.max(m_cur, axis=-1, keepdims=True))
            alpha = jnp.exp2(m_prev - m_new)
            ps = [jnp.exp2((c - m_new).astype(BF16)) for c in cols]
            l_sc[g, rows, :] = alpha * l_sc[g, rows, :] + functools.reduce(jnp.add, ps).astype(F32)
            p = jnp.concatenate(ps, axis=-1)
            acc_sc[g, rows, :] = alpha * acc_sc[g, rows, :] + _dot(p, v_ref[g, pl.ds(start, tk), :])
            m_sc[g, rows, :] = m_new

    def body(j, carry):
        step(j, None)
        return carry

    lax.fori_loop(0, qi * ratio, body, 0)
    for d in range(ratio):
        step(qi * ratio + d, d)
    for g in range(HEAD_GROUP):
        l = jnp.sum(l_sc[g], axis=-1, keepdims=True)
        o_ref[:, g * HEAD_DIM:(g + 1) * HEAD_DIM] = (acc_sc[g] / l).astype(BF16)


def _flash(q, kt, v):
    tq, tk = Q_TILE, ATTN_TILE
    hg = HEAD_GROUP
    return pl.pallas_call(
        _flash_kernel,
        grid=(HEADS // hg, SEQ // tq),
        in_specs=[
            pl.BlockSpec((hg, tq, QK_HEAD), lambda h, i: (h, i, 0)),
            pl.BlockSpec((hg, SEQ // tk, QK_HEAD, tk), lambda h, i: (h, 0, 0, 0), pipeline_mode=pl.Buffered(1)),
            pl.BlockSpec((hg, SEQ, HEAD_DIM), lambda h, i: (h, 0, 0), pipeline_mode=pl.Buffered(1)),
        ],
        out_specs=pl.BlockSpec((tq, hg * HEAD_DIM), lambda h, i: (i, h)),
        out_shape=jax.ShapeDtypeStruct((SEQ, WIDTH), BF16),
        scratch_shapes=[
            pltpu.VMEM((hg, tq, LANES), F32),
            pltpu.VMEM((hg, tq, LANES), F32),
            pltpu.VMEM((hg, tq, HEAD_DIM), F32),
        ],
        compiler_params=_params("arbitrary", "arbitrary"),
        name="mla_flash",
    )(q, kt, v)


def _gelu(x):
    return jax.nn.gelu(x)


def _gmlp_kernel(x_ref, g_ref, wu_ref, wv_ref, lng_ref, lnb_ref, ws_ref, bst_ref, o_ref):
    tm = x_ref.shape[0]
    h = _rms(x_ref[...], g_ref[...]).astype(BF16)
    u = _gelu(_dot(h, wu_ref[...]))
    gv = _gelu(_dot(h, wv_ref[...]))
    mu = jnp.mean(gv, axis=-1, keepdims=True)
    var = jnp.mean(jnp.square(gv - mu), axis=-1, keepdims=True)
    vn = ((gv - mu) * lax.rsqrt(var + EPS) * lng_ref[...] + lnb_ref[...]).astype(BF16)
    row = lax.broadcasted_iota(jnp.int32, (CHUNK, CHUNK), 0)
    col = lax.broadcasted_iota(jnp.int32, (CHUNK, CHUNK), 1)
    bst = bst_ref[...]
    for hd in range(HEADS):
        w = jnp.where(col <= row, ws_ref[hd], 0.0).astype(BF16)
        bias = bst[:, hd:hd + 1]
        cs = slice(hd * HEAD_DIM, (hd + 1) * HEAD_DIM)
        for c in range(tm // CHUNK):
            rs = slice(c * CHUNK, (c + 1) * CHUNK)
            gate = _dot(w, vn[rs, cs]) + bias
            o_ref[rs, cs] = (u[rs, cs] * gate).astype(BF16)


def _gmlp(x, g, wu, wv, lng, lnb, ws, bst, casts=()):
    tm = ROW_TILE
    return _call_with_casts(
        _gmlp_kernel,
        (x, g, wu, wv, lng, lnb, ws, bst),
        grid=(SEQ // tm,),
        in_specs=[
            pl.BlockSpec((tm, D_MODEL), lambda i: (i, 0)),
            _const_spec(g.shape), _const_spec(wu.shape), _const_spec(wv.shape),
            _const_spec(lng.shape), _const_spec(lnb.shape), _const_spec(ws.shape), _const_spec(bst.shape),
        ],
        out_specs=[pl.BlockSpec((tm, WIDTH), lambda i: (i, 0))],
        out_shape=[jax.ShapeDtypeStruct((SEQ, WIDTH), BF16)],
        casts=casts,
        ring_rows=tm,
        compiler_params=_params("arbitrary"),
        name="gmlp",
    )


def _outproj_kernel(x_ref, a_ref, b_ref, wa_ref, wb_ref, o_ref):
    o_ref[...] = x_ref[...] + _dot(a_ref[...], wa_ref[...]) + _dot(b_ref[...], wb_ref[...])


def _outproj(x, a, b, w, casts=()):
    tm = ROW_TILE
    return _call_with_casts(
        _outproj_kernel,
        (x, a, b, w, w),
        grid=(SEQ // tm,),
        in_specs=[
            pl.BlockSpec((tm, D_MODEL), lambda i: (i, 0)),
            pl.BlockSpec((tm, WIDTH), lambda i: (i, 0)),
            pl.BlockSpec((tm, WIDTH), lambda i: (i, 0)),
            pl.BlockSpec((WIDTH, D_MODEL), lambda i: (0, 0), pipeline_mode=pl.Buffered(1)),
            pl.BlockSpec((WIDTH, D_MODEL), lambda i: (1, 0), pipeline_mode=pl.Buffered(1)),
        ],
        out_specs=[pl.BlockSpec((tm, D_MODEL), lambda i: (i, 0))],
        out_shape=[jax.ShapeDtypeStruct((SEQ, D_MODEL), F32)],
        casts=casts,
        ring_rows=tm,
        compiler_params=_params("arbitrary"),
        name="outproj0",
    )


def _ffn_kernel(x_ref, g_ref, wg_ref, wu_ref, wd_ref, o_ref, h_sc):
    @pl.when(pl.program_id(1) == 0)
    def _():
        x = x_ref[...]
        h_sc[...] = _rms(x, g_ref[...]).astype(BF16)
        o_ref[...] = x

    h = h_sc[...]
    half = wg_ref.shape[1] // 2
    acc = None
    for s in range(2):
        cs = slice(s * half, (s + 1) * half)
        a = _dot(h, wg_ref[:, cs])
        b = _dot(h, wu_ref[:, cs])
        t = (a * jax.nn.sigmoid(a) * b).astype(BF16)
        part = _dot(t, wd_ref[cs, :])
        acc = part if acc is None else acc + part
    o_ref[...] += acc


def _ffn(x, g, wg, wu, wd):
    tm, tf = FFN_ROW_TILE, FF_TILE
    return pl.pallas_call(
        _ffn_kernel,
        grid=(SEQ // tm, D_FF // tf),
        in_specs=[
            pl.BlockSpec((tm, D_MODEL), lambda i, j: (i, 0)),
            _const_spec(g.shape),
            pl.BlockSpec((D_MODEL, tf), lambda i, j: (0, j)),
            pl.BlockSpec((D_MODEL, tf), lambda i, j: (0, j)),
            pl.BlockSpec((tf, D_MODEL), lambda i, j: (j, 0)),
        ],
        out_specs=pl.BlockSpec((tm, D_MODEL), lambda i, j: (i, 0)),
        out_shape=jax.ShapeDtypeStruct((SEQ, D_MODEL), F32),
        scratch_shapes=[pltpu.VMEM((tm, D_MODEL), BF16)],
        compiler_params=_params("arbitrary", "arbitrary"),
        name="ffn",
    )(x, g, wg, wu, wd)


def _conv_kernel(x_ref, g_ref, wb_ref, wc_ref, wh_ref, cw_ref, wo_ref, p_ref, gp_ref, wgp_ref, wpp_ref,
                 o_ref, h_sc, tail_sc):
    i = pl.program_id(0)
    j = pl.program_id(1)

    @pl.when(j == 0)
    def _():
        x = x_ref[...]
        hp = _rms(x, gp_ref[...]).astype(BF16)
        gate = jax.nn.sigmoid(_dot(hp, wgp_ref[...]))
        x = x + gate * _dot(p_ref[...].astype(BF16), wpp_ref[...])
        h_sc[...] = _rms(x, g_ref[...]).astype(BF16)
        o_ref[...] = x

    @pl.when(i == 0)
    def _():
        tail_sc[j] = jnp.zeros(tail_sc.shape[1:], F32)

    h = h_sc[...]
    tail = tail_sc[j]
    cw = cw_ref[...]
    half = wb_ref.shape[1] // 2
    acc = None
    for s in range(2):
        cs = slice(s * half, (s + 1) * half)
        gb = _dot(h, wb_ref[:, cs])
        cz = _dot(h, wc_ref[:, cs]) * _dot(h, wh_ref[:, cs])
        prev1 = tail[7:8, cs]
        prev2 = tail[6:7, cs]
        row = lax.broadcasted_iota(jnp.int32, cz.shape, 0)
        z1 = jnp.where(row == 0, prev1, pltpu.roll(cz, 1, axis=0))
        z2 = jnp.where(row == 0, prev2, jnp.where(row == 1, prev1, pltpu.roll(cz, 2, axis=0)))
        y = cw[0:1, cs] * z2 + cw[1:2, cs] * z1 + cw[2:3, cs] * cz
        tail_sc[j, :, cs] = cz[cz.shape[0] - 8:, :]
        part = _dot((gb * y).astype(BF16), wo_ref[cs, :])
        acc = part if acc is None else acc + part
    o_ref[...] += acc


def _conv_mixer(x, g, win, cw, wo, p, gp, wgp, wpp, casts=()):
    tm, tc = ROW_TILE, FF_TILE
    nc = D_MODEL // tc
    return _call_with_casts(
        _conv_kernel,
        (x, g, win, win, win, cw, wo, p, gp, wgp, wpp),
        grid=(SEQ // tm, nc),
        in_specs=[
            pl.BlockSpec((tm, D_MODEL), lambda i, j: (i, 0)),
            _const_spec(g.shape),
            pl.BlockSpec((D_MODEL, tc), lambda i, j: (0, j)),
            pl.BlockSpec((D_MODEL, tc), lambda i, j: (0, j + nc)),
            pl.BlockSpec((D_MODEL, tc), lambda i, j: (0, j + 2 * nc)),
            pl.BlockSpec((8, tc), lambda i, j: (0, j)),
            pl.BlockSpec((tc, D_MODEL), lambda i, j: (j, 0)),
            pl.BlockSpec((None, None, tm, PLE_DIM), lambda i, j: (0, 0, i, 0)),
            _const_spec(gp.shape),
            pl.BlockSpec((None, D_MODEL, D_MODEL), lambda i, j: (0, 0, 0), pipeline_mode=pl.Buffered(1)),
            pl.BlockSpec((None, PLE_DIM, D_MODEL), lambda i, j: (0, 0, 0), pipeline_mode=pl.Buffered(1)),
        ],
        out_specs=[pl.BlockSpec((tm, D_MODEL), lambda i, j: (i, 0))],
        out_shape=[jax.ShapeDtypeStruct((SEQ, D_MODEL), F32)],
        casts=casts,
        scratch_shapes=[pltpu.VMEM((tm, D_MODEL), BF16), pltpu.VMEM((nc, 8, tc), F32)],
        compiler_params=_params("arbitrary", "arbitrary"),
        name="conv_mixer",
    )


def _ple_kernel(x_ref, p_ref, g_ref, wg_ref, wp_ref, gf_ref, o_ref, *, final_norm):
    x = x_ref[...]
    h = _rms(x, g_ref[...]).astype(BF16)
    gate = jax.nn.sigmoid(_dot(h, wg_ref[...]))
    y = x + gate * _dot(p_ref[...].astype(BF16), wp_ref[...])
    if final_norm:
        y = _rms(y, gf_ref[...])
    o_ref[...] = y


def _ple(x, p, g, wg, wp, gf, layer, final_norm):
    tm = PLE_ROW_TILE
    return pl.pallas_call(
        functools.partial(_ple_kernel, final_norm=final_norm),
        grid=(SEQ // tm,),
        in_specs=[
            pl.BlockSpec((tm, D_MODEL), lambda i: (i, 0)),
            pl.BlockSpec((None, None, tm, PLE_DIM), lambda i: (layer, 0, i, 0)),
            _const_spec(g.shape),
            pl.BlockSpec((None, D_MODEL, D_MODEL), lambda i: (layer, 0, 0), pipeline_mode=pl.Buffered(1)),
            pl.BlockSpec((None, PLE_DIM, D_MODEL), lambda i: (layer, 0, 0), pipeline_mode=pl.Buffered(1)),
            _const_spec(gf.shape),
        ],
        out_specs=pl.BlockSpec((tm, D_MODEL), lambda i: (i, 0)),
        out_shape=jax.ShapeDtypeStruct((SEQ, D_MODEL), F32),
        compiler_params=_params("arbitrary"),
        name="ple",
    )(x, p, g, wg, wp, gf)


def _row(v):
    return v.reshape(1, -1)


def kernel(x, p, positions, norm_mix, norm_ffn, norm_ple, w_in0, q_norm, kv_norm, w_qb, w_kvb, v_ln_g,
           v_ln_b, w_spatial, b_spatial, w_out0, w_in1, conv_w, w_out1, w_gate, w_up, w_down, w_ple_gate,
           w_ple_proj, norm_final):
    assert x.shape == (1, SEQ, D_MODEL) and p.shape == (2, 1, SEQ, PLE_DIM)
    xs = x.reshape(SEQ, D_MODEL)
    pos = positions.reshape(SEQ, 1)

    o1, o2, o3 = Q_LORA, Q_LORA + KV_LORA, Q_LORA + KV_LORA + QK_ROPE
    w0 = w_in0[0]
    k1, k2 = w0[:, o2:o2 + ROPE_HALF], w0[:, o2 + ROPE_HALF:o3]
    win_mla = jnp.concatenate([w0[:, :o2], k1, k2, k2, k1], axis=1).astype(BF16)
    wu0 = w0[:, o3:o3 + WIDTH].astype(BF16)
    wv0 = w0[:, o3 + WIDTH:].astype(BF16)
    wq = w_qb[0].reshape(Q_LORA, HEADS, QK_HEAD)
    pe = wq[:, :, HEAD_DIM:]
    pe_swapped = jnp.concatenate([pe[:, :, ROPE_HALF:], pe[:, :, :ROPE_HALF]], axis=-1)
    wqb = jnp.concatenate([wq[:, :, :HEAD_DIM].reshape(Q_LORA, WIDTH), pe.reshape(Q_LORA, HEADS * QK_ROPE),
                           pe_swapped.reshape(Q_LORA, HEADS * QK_ROPE)], axis=1).astype(BF16)
    wkv = w_kvb[0].reshape(KV_LORA, HEADS, 2 * HEAD_DIM)
    wkt = wkv[:, :, :HEAD_DIM].reshape(KV_LORA, WIDTH).T.astype(BF16)
    wv = wkv[:, :, HEAD_DIM:].reshape(KV_LORA, WIDTH).astype(BF16)
    inv_freq = ROPE_BASE ** (-jnp.arange(ROPE_HALF, dtype=F32) / ROPE_HALF)
    inv4 = jnp.tile(inv_freq, LANES // ROPE_HALF).reshape(1, LANES)
    sgn4 = jnp.tile(jnp.concatenate([-jnp.ones(ROPE_HALF, F32), jnp.ones(ROPE_HALF, F32)]), 2).reshape(1, LANES)
    cw = jnp.concatenate([conv_w[0], jnp.zeros((8 - CONV_W, D_MODEL), F32)], axis=0)
    wp_ple = w_ple_proj.astype(BF16)

    n_attn, n_row = SEQ // ATTN_TILE, SEQ // ROW_TILE

    def slab_cast(w, layer, steps):
        rows, cols = w.shape[1] // steps, w.shape[2]
        return _Cast(w, pl.BlockSpec((None, rows, cols), lambda i: (layer, i, 0)),
                     pl.BlockSpec((rows, cols), lambda i: (i, 0)),
                     jax.ShapeDtypeStruct(w.shape[1:], BF16))

    def tile_cast(w, layer):
        blk = (w.shape[1] // n_row, w.shape[2] // (D_MODEL // FF_TILE))
        return _Cast(w, pl.BlockSpec((None,) + blk, lambda i, j: (layer, i, j)),
                     pl.BlockSpec(blk, lambda i, j: (i, j)), jax.ShapeDtypeStruct(w.shape[1:], BF16))

    ple_rows = D_MODEL // n_row
    ple_gate_cast = _Cast(w_ple_gate, pl.BlockSpec((2, ple_rows, D_MODEL), lambda i: (0, i, 0)),
                          pl.BlockSpec((2, ple_rows, D_MODEL), lambda i: (0, i, 0)),
                          jax.ShapeDtypeStruct(w_ple_gate.shape, BF16))

    (q, kt, v), (wg0, wu0_ffn) = _mla_pre(
        xs, pos, _row(norm_mix[0]), win_mla, _row(q_norm[0]), _row(kv_norm[0]), wqb, wkt, wv, inv4, sgn4,
        casts=[slab_cast(w_gate, 0, n_attn), slab_cast(w_up, 0, n_attn)])
    attn = _flash(q, kt, v)
    (gm,), (wd0, win1, wo0) = _gmlp(
        xs, _row(norm_mix[0]), wu0, wv0, _row(v_ln_g[0]), _row(v_ln_b[0]), w_spatial[0], b_spatial[0].T,
        casts=[slab_cast(w_down, 0, n_row), slab_cast(w_in1, 0, n_row), slab_cast(w_out0, 0, n_row)])
    (xs,), (wo1, wg_ple) = _outproj(xs, attn, gm, wo0, casts=[slab_cast(w_out1, 0, n_row), ple_gate_cast])
    xs = _ffn(xs, _row(norm_ffn[0]), wg0, wu0_ffn, wd0)

    (xs,), (wg1, wu1, wd1) = _conv_mixer(
        xs, _row(norm_mix[1]), win1, cw, wo1, p, _row(norm_ple[0]), wg_ple, wp_ple,
        casts=[tile_cast(w_gate, 1), tile_cast(w_up, 1), tile_cast(w_down, 1)])
    xs = _ffn(xs, _row(norm_ffn[1]), wg1, wu1, wd1)
    xs = _ple(xs, p, _row(norm_ple[1]), wg_ple, wp_ple, _row(norm_final), layer=1, final_norm=True)
    return xs.reshape(1, SEQ, D_MODEL)
```

```python
import functools
import math
from typing import NamedTuple

import jax
import jax.numpy as jnp
from jax import lax
from jax.experimental import pallas as pl
from jax.experimental.pallas import tpu as pltpu

F32 = jnp.float32
BF16 = jnp.bfloat16

D_MODEL = 2048
SEQ = 16384
EPS = 1e-6
PLE_DIM = 256
HEADS = 8
HEAD_DIM = 128
CHUNK = 128
Q_LORA = 512
KV_LORA = 256
QK_ROPE = 64
ROPE_HALF = QK_ROPE // 2
QK_HEAD = HEAD_DIM + QK_ROPE
WIDTH = HEADS * HEAD_DIM
ROPE_WIDTH = HEADS * QK_ROPE
ROPE_BASE = 10000.0
D_FF = 5632
CONV_W = 3

VMEM_LIMIT_BYTES = 56 * 1024 * 1024
LANES = 128

ROW_TILE = 512
FFN_ROW_TILE = 1024
ATTN_TILE = 512
Q_TILE = 1024
HEAD_GROUP = 8
PLE_ROW_TILE = 1024
FF_TILE = 512
Q_SCALE = (QK_HEAD ** -0.5) * math.log2(math.e)
MASK_VALUE = -1e30


def _params(*semantics):
    return pltpu.CompilerParams(dimension_semantics=semantics, vmem_limit_bytes=VMEM_LIMIT_BYTES)


def _const_spec(shape):
    zeros = (0,) * len(shape)
    return pl.BlockSpec(shape, lambda *_: zeros, pipeline_mode=pl.Buffered(1))


class _Cast(NamedTuple):
    src: jax.Array
    in_spec: pl.BlockSpec
    out_spec: pl.BlockSpec
    out_shape: jax.ShapeDtypeStruct


def _with_casts(body, n_in, n_out, n_cast):
    def kern(*refs):
        ins = refs[:n_in]
        cast_ins = refs[n_in:n_in + n_cast]
        outs = refs[n_in + n_cast:n_in + n_cast + n_out]
        cast_outs = refs[n_in + n_cast + n_out:n_in + 2 * n_cast + n_out]
        scratch = refs[n_in + 2 * n_cast + n_out:]
        for s, d in zip(cast_ins, cast_outs):
            d[...] = s[...].astype(BF16)
        body(*ins, *outs, *scratch)
    return kern


def _call_with_casts(body, inputs, in_specs, out_specs, out_shape, casts, **kwargs):
    casts = tuple(casts)
    n_out = len(out_specs)
    res = pl.pallas_call(
        _with_casts(body, len(inputs), n_out, len(casts)),
        in_specs=list(in_specs) + [c.in_spec for c in casts],
        out_specs=list(out_specs) + [c.out_spec for c in casts],
        out_shape=list(out_shape) + [c.out_shape for c in casts],
        **kwargs,
    )(*inputs, *[c.src for c in casts])
    return res[:n_out], res[n_out:]


def _rms(x, g):
    return x * lax.rsqrt(jnp.mean(x * x, axis=-1, keepdims=True) + EPS) * g


def _dot(a, b):
    return jnp.dot(a, b, preferred_element_type=F32)


def _dot_nt(a, b):
    return lax.dot_general(a, b, (((1,), (1,)), ((), ())), preferred_element_type=F32)


def _mla_pre_kernel(x_ref, pos_ref, g_ref, win_ref, qn_ref, kvn_ref, wqb_ref, wkt_ref, wv_ref,
                    inv_ref, sgn_ref, q_ref, kt_ref, v_ref):
    h = _rms(x_ref[...], g_ref[...]).astype(BF16)
    z = _dot(h, win_ref[...])
    q_lat = z[:, :Q_LORA]
    kv_lat = z[:, Q_LORA:Q_LORA + KV_LORA]
    kk = z[:, Q_LORA + KV_LORA:]

    ang = pos_ref[...].astype(F32) * inv_ref[...]
    cos = jnp.cos(ang)
    sin = jnp.sin(ang) * sgn_ref[...]

    qn = _rms(q_lat, qn_ref[...]).astype(BF16)
    qf = _dot(qn, wqb_ref[...])
    cos4 = jnp.concatenate([cos] * 4, axis=-1)
    sin4 = jnp.concatenate([sin] * 4, axis=-1)
    q_pe = qf[:, WIDTH:WIDTH + ROPE_WIDTH] * cos4 + qf[:, WIDTH + ROPE_WIDTH:] * sin4
    for hd in range(HEADS):
        q_ref[hd, :, 0:HEAD_DIM] = (qf[:, hd * HEAD_DIM:(hd + 1) * HEAD_DIM] * Q_SCALE).astype(BF16)
        q_ref[hd, :, HEAD_DIM:QK_HEAD] = (q_pe[:, hd * QK_ROPE:(hd + 1) * QK_ROPE] * Q_SCALE).astype(BF16)

    lane = lax.broadcasted_iota(jnp.int32, ang.shape, 1)
    t = kk * jnp.where(lane < QK_ROPE, cos, sin)
    k_pe = t + pltpu.roll(t, QK_ROPE, axis=1)
    k_pe_t = k_pe.T[:QK_ROPE, :].astype(BF16)

    kvn = _rms(kv_lat, kvn_ref[...]).astype(BF16)
    kn_t = _dot_nt(wkt_ref[...], kvn)
    vv = _dot(kvn, wv_ref[...])
    for hd in range(HEADS):
        kt_ref[hd, 0, 0:HEAD_DIM, :] = kn_t[hd * HEAD_DIM:(hd + 1) * HEAD_DIM, :].astype(BF16)
        kt_ref[hd, 0, HEAD_DIM:QK_HEAD, :] = k_pe_t
        v_ref[hd] = vv[:, hd * HEAD_DIM:(hd + 1) * HEAD_DIM].astype(BF16)


def _mla_pre(x, pos, g, win, qn, kvn, wqb, wkt, wv, inv4, sgn4, casts=()):
    tm = ATTN_TILE
    n = SEQ // tm
    return _call_with_casts(
        _mla_pre_kernel,
        (x, pos, g, win, qn, kvn, wqb, wkt, wv, inv4, sgn4),
        grid=(n,),
        in_specs=[
            pl.BlockSpec((tm, D_MODEL), lambda i: (i, 0)),
            pl.BlockSpec((tm, 1), lambda i: (i, 0)),
            _const_spec(g.shape), _const_spec(win.shape), _const_spec(qn.shape), _const_spec(kvn.shape),
            _const_spec(wqb.shape), _const_spec(wkt.shape), _const_spec(wv.shape),
            _const_spec(inv4.shape), _const_spec(sgn4.shape),
        ],
        out_specs=[
            pl.BlockSpec((HEADS, tm, QK_HEAD), lambda i: (0, i, 0)),
            pl.BlockSpec((HEADS, 1, QK_HEAD, tm), lambda i: (0, i, 0, 0)),
            pl.BlockSpec((HEADS, tm, HEAD_DIM), lambda i: (0, i, 0)),
        ],
        out_shape=[
            jax.ShapeDtypeStruct((HEADS, SEQ, QK_HEAD), BF16),
            jax.ShapeDtypeStruct((HEADS, n, QK_HEAD, tm), BF16),
            jax.ShapeDtypeStruct((HEADS, SEQ, HEAD_DIM), BF16),
        ],
        casts=casts,
        compiler_params=_params("arbitrary"),
        name="mla_pre",
    )


def _flash_kernel(q_ref, kt_hbm, v_hbm, o_ref, m_sc, l_sc, acc_sc, k_buf, v_buf, sem):
    hi = pl.program_id(0)
    qi = pl.program_id(1)
    n_h = pl.num_programs(0)
    n_q = pl.num_programs(1)
    tq, tk = Q_TILE, ATTN_TILE
    ratio = tq // tk
    ncol = tk // LANES

    def kv_copies(group, j, slot):
        hs = pl.ds(group * HEAD_GROUP, HEAD_GROUP)
        rows = pl.ds(pl.multiple_of(j * tk, tk), tk)
        return (pltpu.make_async_copy(kt_hbm.at[hs, j], k_buf.at[slot], sem.at[slot, 0]),
                pltpu.make_async_copy(v_hbm.at[hs, rows, :], v_buf.at[slot], sem.at[slot, 1]))

    def start_kv(group, j, slot):
        for c in kv_copies(group, j, slot):
            c.start()

    @pl.when(jnp.logical_and(hi == 0, qi == 0))
    def _():
        start_kv(0, 0, 0)

    m_sc[...] = jnp.full(m_sc.shape, MASK_VALUE, F32)
    l_sc[...] = jnp.zeros(l_sc.shape, F32)
    acc_sc[...] = jnp.zeros(acc_sc.shape, F32)

    def step(j, diag):
        slot = j % 2
        for c in kv_copies(hi, j, slot):
            c.wait()
        if diag is None or diag + 1 < ratio:
            start_kv(hi, j + 1, 1 - slot)
        else:
            last_q = qi + 1 == n_q

            @pl.when(jnp.logical_not(last_q))
            def _():
                start_kv(hi, 0, 0)

            @pl.when(jnp.logical_and(last_q, hi + 1 < n_h))
            def _():
                start_kv(hi + 1, 0, 0)

        rows = slice(None) if diag is None else slice(diag * tk, tq)
        for g in range(HEAD_GROUP):
            s = _dot(q_ref[g, rows, :], k_buf[slot, g])
            if diag is not None:
                row = lax.broadcasted_iota(jnp.int32, s.shape, 0)
                col = lax.broadcasted_iota(jnp.int32, s.shape, 1)
                s = jnp.where(col <= row, s, MASK_VALUE)
            cols = [s[:, c * LANES:(c + 1) * LANES] for c in range(ncol)]
            m_prev = m_sc[g, rows, :]
            m_cur = functools.reduce(jnp.maximum, cols)
            m_new = jnp.maximum(m_prev, jnp.max(m_cur, axis=-1, keepdims=True))
            alpha = jnp.exp2(m_prev - m_new)
            ps = [jnp.exp2((c - m_new).astype(BF16)) for c in cols]
            l_sc[g, rows, :] = alpha * l_sc[g, rows, :] + functools.reduce(jnp.add, ps).astype(F32)
            p = jnp.concatenate(ps, axis=-1)
            acc_sc[g, rows, :] = alpha * acc_sc[g, rows, :] + _dot(p, v_buf[slot, g])
            m_sc[g, rows, :] = m_new

    def body(j, carry):
        step(j, None)
        return carry

    lax.fori_loop(0, qi * ratio, body, 0)
    for d in range(ratio):
        step(qi * ratio + d, d)
    for g in range(HEAD_GROUP):
        l = jnp.sum(l_sc[g], axis=-1, keepdims=True)
        o_ref[:, g * HEAD_DIM:(g + 1) * HEAD_DIM] = (acc_sc[g] / l).astype(BF16)


def _flash(q, kt, v):
    tq, tk = Q_TILE, ATTN_TILE
    hg = HEAD_GROUP
    assert (tq // tk) % 2 == 0
    return pl.pallas_call(
        _flash_kernel,
        grid=(HEADS // hg, SEQ // tq),
        in_specs=[
            pl.BlockSpec((hg, tq, QK_HEAD), lambda h, i: (h, i, 0)),
            pl.BlockSpec(memory_space=pl.ANY),
            pl.BlockSpec(memory_space=pl.ANY),
        ],
        out_specs=pl.BlockSpec((tq, hg * HEAD_DIM), lambda h, i: (i, h)),
        out_shape=jax.ShapeDtypeStruct((SEQ, WIDTH), BF16),
        scratch_shapes=[
            pltpu.VMEM((hg, tq, LANES), F32),
            pltpu.VMEM((hg, tq, LANES), F32),
            pltpu.VMEM((hg, tq, HEAD_DIM), F32),
            pltpu.VMEM((2, hg, QK_HEAD, tk), BF16),
            pltpu.VMEM((2, hg, tk, HEAD_DIM), BF16),
            pltpu.SemaphoreType.DMA((2, 2)),
        ],
        compiler_params=_params("arbitrary", "arbitrary"),
        name="mla_flash",
    )(q, kt, v)


def _gelu(x):
    return jax.nn.gelu(x)


def _gmlp_kernel(x_ref, g_ref, wu_ref, wv_ref, lng_ref, lnb_ref, ws_ref, bst_ref, o_ref):
    tm = x_ref.shape[0]
    h = _rms(x_ref[...], g_ref[...]).astype(BF16)
    u = _gelu(_dot(h, wu_ref[...]))
    gv = _gelu(_dot(h, wv_ref[...]))
    mu = jnp.mean(gv, axis=-1, keepdims=True)
    var = jnp.mean(jnp.square(gv - mu), axis=-1, keepdims=True)
    vn = ((gv - mu) * lax.rsqrt(var + EPS) * lng_ref[...] + lnb_ref[...]).astype(BF16)
    row = lax.broadcasted_iota(jnp.int32, (CHUNK, CHUNK), 0)
    col = lax.broadcasted_iota(jnp.int32, (CHUNK, CHUNK), 1)
    bst = bst_ref[...]
    for hd in range(HEADS):
        w = jnp.where(col <= row, ws_ref[hd], 0.0).astype(BF16)
        bias = bst[:, hd:hd + 1]
        cs = slice(hd * HEAD_DIM, (hd + 1) * HEAD_DIM)
        for c in range(tm // CHUNK):
            rs = slice(c * CHUNK, (c + 1) * CHUNK)
            gate = _dot(w, vn[rs, cs]) + bias
            o_ref[rs, cs] = (u[rs, cs] * gate).astype(BF16)


def _gmlp(x, g, wu, wv, lng, lnb, ws, bst, casts=()):
    tm = ROW_TILE
    return _call_with_casts(
        _gmlp_kernel,
        (x, g, wu, wv, lng, lnb, ws, bst),
        grid=(SEQ // tm,),
        in_specs=[
            pl.BlockSpec((tm, D_MODEL), lambda i: (i, 0)),
            _const_spec(g.shape), _const_spec(wu.shape), _const_spec(wv.shape),
            _const_spec(lng.shape), _const_spec(lnb.shape), _const_spec(ws.shape), _const_spec(bst.shape),
        ],
        out_specs=[pl.BlockSpec((tm, WIDTH), lambda i: (i, 0))],
        out_shape=[jax.ShapeDtypeStruct((SEQ, WIDTH), BF16)],
        casts=casts,
        compiler_params=_params("arbitrary"),
        name="gmlp",
    )


def _outproj_kernel(x_ref, a_ref, b_ref, wa_ref, wb_ref, o_ref):
    o_ref[...] = x_ref[...] + _dot(a_ref[...], wa_ref[...]) + _dot(b_ref[...], wb_ref[...])


def _outproj(x, a, b, w, casts=()):
    tm = ROW_TILE
    return _call_with_casts(
        _outproj_kernel,
        (x, a, b, w, w),
        grid=(SEQ // tm,),
        in_specs=[
            pl.BlockSpec((tm, D_MODEL), lambda i: (i, 0)),
            pl.BlockSpec((tm, WIDTH), lambda i: (i, 0)),
            pl.BlockSpec((tm, WIDTH), lambda i: (i, 0)),
            pl.BlockSpec((WIDTH, D_MODEL), lambda i: (0, 0), pipeline_mode=pl.Buffered(1)),
            pl.BlockSpec((WIDTH, D_MODEL), lambda i: (1, 0), pipeline_mode=pl.Buffered(1)),
        ],
        out_specs=[pl.BlockSpec((tm, D_MODEL), lambda i: (i, 0))],
        out_shape=[jax.ShapeDtypeStruct((SEQ, D_MODEL), F32)],
        casts=casts,
        compiler_params=_params("arbitrary"),
        name="outproj0",
    )


def _ffn_kernel(x_ref, g_ref, wg_ref, wu_ref, wd_ref, o_ref, h_sc):
    @pl.when(pl.program_id(1) == 0)
    def _():
        x = x_ref[...]
        h_sc[...] = _rms(x, g_ref[...]).astype(BF16)
        o_ref[...] = x

    h = h_sc[...]
    half = wg_ref.shape[1] // 2
    acc = None
    for s in range(2):
        cs = slice(s * half, (s + 1) * half)
        a = _dot(h, wg_ref[:, cs])
        b = _dot(h, wu_ref[:, cs])
        t = (a * jax.nn.sigmoid(a) * b).astype(BF16)
        part = _dot(t, wd_ref[cs, :])
        acc = part if acc is None else acc + part
    o_ref[...] += acc


def _ffn(x, g, wg, wu, wd):
    tm, tf = FFN_ROW_TILE, FF_TILE
    return pl.pallas_call(
        _ffn_kernel,
        grid=(SEQ // tm, D_FF // tf),
        in_specs=[
            pl.BlockSpec((tm, D_MODEL), lambda i, j: (i, 0)),
            _const_spec(g.shape),
            pl.BlockSpec((D_MODEL, tf), lambda i, j: (0, j)),
            pl.BlockSpec((D_MODEL, tf), lambda i, j: (0, j)),
            pl.BlockSpec((tf, D_MODEL), lambda i, j: (j, 0)),
        ],
        out_specs=pl.BlockSpec((tm, D_MODEL), lambda i, j: (i, 0)),
        out_shape=jax.ShapeDtypeStruct((SEQ, D_MODEL), F32),
        scratch_shapes=[pltpu.VMEM((tm, D_MODEL), BF16)],
        compiler_params=_params("arbitrary", "arbitrary"),
        name="ffn",
    )(x, g, wg, wu, wd)


def _conv_kernel(x_ref, g_ref, wb_ref, wc_ref, wh_ref, cw_ref, wo_ref, p_ref, gp_ref, wgp_ref, wpp_ref,
                 o_ref, h_sc, tail_sc):
    i = pl.program_id(0)
    j = pl.program_id(1)

    @pl.when(j == 0)
    def _():
        x = x_ref[...]
        hp = _rms(x, gp_ref[...]).astype(BF16)
        gate = jax.nn.sigmoid(_dot(hp, wgp_ref[...]))
        x = x + gate * _dot(p_ref[...].astype(BF16), wpp_ref[...])
        h_sc[...] = _rms(x, g_ref[...]).astype(BF16)
        o_ref[...] = x

    @pl.when(i == 0)
    def _():
        tail_sc[j] = jnp.zeros(tail_sc.shape[1:], F32)

    h = h_sc[...]
    tail = tail_sc[j]
    cw = cw_ref[...]
    half = wb_ref.shape[1] // 2
    acc = None
    for s in range(2):
        cs = slice(s * half, (s + 1) * half)
        gb = _dot(h, wb_ref[:, cs])
        cz = _dot(h, wc_ref[:, cs]) * _dot(h, wh_ref[:, cs])
        prev1 = tail[7:8, cs]
        prev2 = tail[6:7, cs]
        row = lax.broadcasted_iota(jnp.int32, cz.shape, 0)
        z1 = jnp.where(row == 0, prev1, pltpu.roll(cz, 1, axis=0))
        z2 = jnp.where(row == 0, prev2, jnp.where(row == 1, prev1, pltpu.roll(cz, 2, axis=0)))
        y = cw[0:1, cs] * z2 + cw[1:2, cs] * z1 + cw[2:3, cs] * cz
        tail_sc[j, :, cs] = cz[cz.shape[0] - 8:, :]
        part = _dot((gb * y).astype(BF16), wo_ref[cs, :])
        acc = part if acc is None else acc + part
    o_ref[...] += acc


def _conv_mixer(x, g, win, cw, wo, p, gp, wgp, wpp, casts=()):
    tm, tc = ROW_TILE, FF_TILE
    nc = D_MODEL // tc
    return _call_with_casts(
        _conv_kernel,
        (x, g, win, win, win, cw, wo, p, gp, wgp, wpp),
        grid=(SEQ // tm, nc),
        in_specs=[
            pl.BlockSpec((tm, D_MODEL), lambda i, j: (i, 0)),
            _const_spec(g.shape),
            pl.BlockSpec((D_MODEL, tc), lambda i, j: (0, j)),
            pl.BlockSpec((D_MODEL, tc), lambda i, j: (0, j + nc)),
            pl.BlockSpec((D_MODEL, tc), lambda i, j: (0, j + 2 * nc)),
            pl.BlockSpec((8, tc), lambda i, j: (0, j)),
            pl.BlockSpec((tc, D_MODEL), lambda i, j: (j, 0)),
            pl.BlockSpec((None, None, tm, PLE_DIM), lambda i, j: (0, 0, i, 0)),
            _const_spec(gp.shape),
            pl.BlockSpec((None, D_MODEL, D_MODEL), lambda i, j: (0, 0, 0), pipeline_mode=pl.Buffered(1)),
            pl.BlockSpec((None, PLE_DIM, D_MODEL), lambda i, j: (0, 0, 0), pipeline_mode=pl.Buffered(1)),
        ],
        out_specs=[pl.BlockSpec((tm, D_MODEL), lambda i, j: (i, 0))],
        out_shape=[jax.ShapeDtypeStruct((SEQ, D_MODEL), F32)],
        casts=casts,
        scratch_shapes=[pltpu.VMEM((tm, D_MODEL), BF16), pltpu.VMEM((nc, 8, tc), F32)],
        compiler_params=_params("arbitrary", "arbitrary"),
        name="conv_mixer",
    )


def _ple_kernel(x_ref, p_ref, g_ref, wg_ref, wp_ref, gf_ref, o_ref, *, final_norm):
    x = x_ref[...]
    h = _rms(x, g_ref[...]).astype(BF16)
    gate = jax.nn.sigmoid(_dot(h, wg_ref[...]))
    y = x + gate * _dot(p_ref[...].astype(BF16), wp_ref[...])
    if final_norm:
        y = _rms(y, gf_ref[...])
    o_ref[...] = y


def _ple(x, p, g, wg, wp, gf, layer, final_norm):
    tm = PLE_ROW_TILE
    return pl.pallas_call(
        functools.partial(_ple_kernel, final_norm=final_norm),
        grid=(SEQ // tm,),
        in_specs=[
            pl.BlockSpec((tm, D_MODEL), lambda i: (i, 0)),
            pl.BlockSpec((None, None, tm, PLE_DIM), lambda i: (layer, 0, i, 0)),
            _const_spec(g.shape),
            pl.BlockSpec((None, D_MODEL, D_MODEL), lambda i: (layer, 0, 0), pipeline_mode=pl.Buffered(1)),
            pl.BlockSpec((None, PLE_DIM, D_MODEL), lambda i: (layer, 0, 0), pipeline_mode=pl.Buffered(1)),
            _const_spec(gf.shape),
        ],
        out_specs=pl.BlockSpec((tm, D_MODEL), lambda i: (i, 0)),
        out_shape=jax.ShapeDtypeStruct((SEQ, D_MODEL), F32),
        compiler_params=_params("arbitrary"),
        name="ple",
    )(x, p, g, wg, wp, gf)


def _row(v):
    return v.reshape(1, -1)


def kernel(x, p, positions, norm_mix, norm_ffn, norm_ple, w_in0, q_norm, kv_norm, w_qb, w_kvb, v_ln_g,
           v_ln_b, w_spatial, b_spatial, w_out0, w_in1, conv_w, w_out1, w_gate, w_up, w_down, w_ple_gate,
           w_ple_proj, norm_final):
    assert x.shape == (1, SEQ, D_MODEL) and p.shape == (2, 1, SEQ, PLE_DIM)
    xs = x.reshape(SEQ, D_MODEL)
    pos = positions.reshape(SEQ, 1)

    o1, o2, o3 = Q_LORA, Q_LORA + KV_LORA, Q_LORA + KV_LORA + QK_ROPE
    w0 = w_in0[0]
    k1, k2 = w0[:, o2:o2 + ROPE_HALF], w0[:, o2 + ROPE_HALF:o3]
    win_mla = jnp.concatenate([w0[:, :o2], k1, k2, k2, k1], axis=1).astype(BF16)
    wu0 = w0[:, o3:o3 + WIDTH].astype(BF16)
    wv0 = w0[:, o3 + WIDTH:].astype(BF16)
    wq = w_qb[0].reshape(Q_LORA, HEADS, QK_HEAD)
    pe = wq[:, :, HEAD_DIM:]
    pe_swapped = jnp.concatenate([pe[:, :, ROPE_HALF:], pe[:, :, :ROPE_HALF]], axis=-1)
    wqb = jnp.concatenate([wq[:, :, :HEAD_DIM].reshape(Q_LORA, WIDTH), pe.reshape(Q_LORA, HEADS * QK_ROPE),
                           pe_swapped.reshape(Q_LORA, HEADS * QK_ROPE)], axis=1).astype(BF16)
    wkv = w_kvb[0].reshape(KV_LORA, HEADS, 2 * HEAD_DIM)
    wkt = wkv[:, :, :HEAD_DIM].reshape(KV_LORA, WIDTH).T.astype(BF16)
    wv = wkv[:, :, HEAD_DIM:].reshape(KV_LORA, WIDTH).astype(BF16)
    inv_freq = ROPE_BASE ** (-jnp.arange(ROPE_HALF, dtype=F32) / ROPE_HALF)
    inv4 = jnp.tile(inv_freq, LANES // ROPE_HALF).reshape(1, LANES)
    sgn4 = jnp.tile(jnp.concatenate([-jnp.ones(ROPE_HALF, F32), jnp.ones(ROPE_HALF, F32)]), 2).reshape(1, LANES)
    cw = jnp.concatenate([conv_w[0], jnp.zeros((8 - CONV_W, D_MODEL), F32)], axis=0)
    wp_ple = w_ple_proj.astype(BF16)

    n_attn, n_row = SEQ // ATTN_TILE, SEQ // ROW_TILE

    def slab_cast(w, layer, steps):
        rows, cols = w.shape[1] // steps, w.shape[2]
        return _Cast(w, pl.BlockSpec((None, rows, cols), lambda i: (layer, i, 0)),
                     pl.BlockSpec((rows, cols), lambda i: (i, 0)),
                     jax.ShapeDtypeStruct(w.shape[1:], BF16))

    def tile_cast(w, layer):
        blk = (w.shape[1] // n_row, w.shape[2] // (D_MODEL // FF_TILE))
        return _Cast(w, pl.BlockSpec((None,) + blk, lambda i, j: (layer, i, j)),
                     pl.BlockSpec(blk, lambda i, j: (i, j)), jax.ShapeDtypeStruct(w.shape[1:], BF16))

    ple_rows = D_MODEL // n_row
    ple_gate_cast = _Cast(w_ple_gate, pl.BlockSpec((2, ple_rows, D_MODEL), lambda i: (0, i, 0)),
                          pl.BlockSpec((2, ple_rows, D_MODEL), lambda i: (0, i, 0)),
                          jax.ShapeDtypeStruct(w_ple_gate.shape, BF16))

    (q, kt, v), (wg0, wu0_ffn) = _mla_pre(
        xs, pos, _row(norm_mix[0]), win_mla, _row(q_norm[0]), _row(kv_norm[0]), wqb, wkt, wv, inv4, sgn4,
        casts=[slab_cast(w_gate, 0, n_attn), slab_cast(w_up, 0, n_attn)])
    attn = _flash(q, kt, v)
    (gm,), (wd0, win1, wo0) = _gmlp(
        xs, _row(norm_mix[0]), wu0, wv0, _row(v_ln_g[0]), _row(v_ln_b[0]), w_spatial[0], b_spatial[0].T,
        casts=[slab_cast(w_down, 0, n_row), slab_cast(w_in1, 0, n_row), slab_cast(w_out0, 0, n_row)])
    (xs,), (wo1, wg_ple) = _outproj(xs, attn, gm, wo0, casts=[slab_cast(w_out1, 0, n_row), ple_gate_cast])
    xs = _ffn(xs, _row(norm_ffn[0]), wg0, wu0_ffn, wd0)

    (xs,), (wg1, wu1, wd1) = _conv_mixer(
        xs, _row(norm_mix[1]), win1, cw, wo1, p, _row(norm_ple[0]), wg_ple, wp_ple,
        casts=[tile_cast(w_gate, 1), tile_cast(w_up, 1), tile_cast(w_down, 1)])
    xs = _ffn(xs, _row(norm_ffn[1]), wg1, wu1, wd1)
    xs = _ple(xs, p, _row(norm_ple[1]), wg_ple, wp_ple, _row(norm_final), layer=1, final_norm=True)
    return xs.reshape(1, SEQ, D_MODEL)
```

```python
import functools
import math
from typing import NamedTuple

import jax
import jax.numpy as jnp
from jax import lax
from jax.experimental import pallas as pl
from jax.experimental.pallas import tpu as pltpu

F32 = jnp.float32
BF16 = jnp.bfloat16

D_MODEL = 2048
SEQ = 16384
EPS = 1e-6
PLE_DIM = 256
HEADS = 8
HEAD_DIM = 128
CHUNK = 128
Q_LORA = 512
KV_LORA = 256
QK_ROPE = 64
ROPE_HALF = QK_ROPE // 2
QK_HEAD = HEAD_DIM + QK_ROPE
WIDTH = HEADS * HEAD_DIM
ROPE_WIDTH = HEADS * QK_ROPE
ROPE_BASE = 10000.0
D_FF = 5632
CONV_W = 3

VMEM_LIMIT_BYTES = 56 * 1024 * 1024
LANES = 128

ROW_TILE = 512
FFN_ROW_TILE = 1024
ATTN_TILE = 512
Q_TILE = 1024
HEAD_GROUP = 8
PLE_ROW_TILE = 1024
FF_TILE = 512
Q_SCALE = (QK_HEAD ** -0.5) * math.log2(math.e)
MASK_VALUE = -1e30


def _params(*semantics):
    return pltpu.CompilerParams(dimension_semantics=semantics, vmem_limit_bytes=VMEM_LIMIT_BYTES)


def _const_spec(shape):
    zeros = (0,) * len(shape)
    return pl.BlockSpec(shape, lambda *_: zeros, pipeline_mode=pl.Buffered(1))


class _Cast(NamedTuple):
    src: jax.Array
    in_spec: pl.BlockSpec
    out_spec: pl.BlockSpec
    out_shape: jax.ShapeDtypeStruct


def _with_casts(body, n_in, n_out, n_cast):
    def kern(*refs):
        ins = refs[:n_in]
        cast_ins = refs[n_in:n_in + n_cast]
        outs = refs[n_in + n_cast:n_in + n_cast + n_out]
        cast_outs = refs[n_in + n_cast + n_out:n_in + 2 * n_cast + n_out]
        scratch = refs[n_in + 2 * n_cast + n_out:]
        for s, d in zip(cast_ins, cast_outs):
            d[...] = s[...].astype(BF16)
        body(*ins, *outs, *scratch)
    return kern


def _call_with_casts(body, inputs, in_specs, out_specs, out_shape, casts, **kwargs):
    casts = tuple(casts)
    n_out = len(out_specs)
    res = pl.pallas_call(
        _with_casts(body, len(inputs), n_out, len(casts)),
        in_specs=list(in_specs) + [c.in_spec for c in casts],
        out_specs=list(out_specs) + [c.out_spec for c in casts],
        out_shape=list(out_shape) + [c.out_shape for c in casts],
        **kwargs,
    )(*inputs, *[c.src for c in casts])
    return res[:n_out], res[n_out:]


def _rms(x, g):
    return x * lax.rsqrt(jnp.mean(x * x, axis=-1, keepdims=True) + EPS) * g


def _dot(a, b):
    return jnp.dot(a, b, preferred_element_type=F32)


def _dot_nt(a, b):
    return lax.dot_general(a, b, (((1,), (1,)), ((), ())), preferred_element_type=F32)


def _mla_pre_kernel(x_ref, pos_ref, g_ref, win_ref, qn_ref, kvn_ref, wqb_ref, wkt_ref, wv_ref,
                    inv_ref, sgn_ref, q_ref, kt_ref, v_ref):
    h = _rms(x_ref[...], g_ref[...]).astype(BF16)
    z = _dot(h, win_ref[...])
    q_lat = z[:, :Q_LORA]
    kv_lat = z[:, Q_LORA:Q_LORA + KV_LORA]
    kk = z[:, Q_LORA + KV_LORA:]

    ang = pos_ref[...].astype(F32) * inv_ref[...]
    cos = jnp.cos(ang)
    sin = jnp.sin(ang) * sgn_ref[...]

    qn = _rms(q_lat, qn_ref[...]).astype(BF16)
    qf = _dot(qn, wqb_ref[...])
    cos4 = jnp.concatenate([cos] * 4, axis=-1)
    sin4 = jnp.concatenate([sin] * 4, axis=-1)
    q_pe = qf[:, WIDTH:WIDTH + ROPE_WIDTH] * cos4 + qf[:, WIDTH + ROPE_WIDTH:] * sin4
    for hd in range(HEADS):
        q_ref[hd, :, 0:HEAD_DIM] = (qf[:, hd * HEAD_DIM:(hd + 1) * HEAD_DIM] * Q_SCALE).astype(BF16)
        q_ref[hd, :, HEAD_DIM:QK_HEAD] = (q_pe[:, hd * QK_ROPE:(hd + 1) * QK_ROPE] * Q_SCALE).astype(BF16)

    lane = lax.broadcasted_iota(jnp.int32, ang.shape, 1)
    t = kk * jnp.where(lane < QK_ROPE, cos, sin)
    k_pe = t + pltpu.roll(t, QK_ROPE, axis=1)
    k_pe_t = k_pe.T[:QK_ROPE, :].astype(BF16)

    kvn = _rms(kv_lat, kvn_ref[...]).astype(BF16)
    kn_t = _dot_nt(wkt_ref[...], kvn)
    vv = _dot(kvn, wv_ref[...])
    for hd in range(HEADS):
        kt_ref[hd, 0, 0:HEAD_DIM, :] = kn_t[hd * HEAD_DIM:(hd + 1) * HEAD_DIM, :].astype(BF16)
        kt_ref[hd, 0, HEAD_DIM:QK_HEAD, :] = k_pe_t
        v_ref[hd] = vv[:, hd * HEAD_DIM:(hd + 1) * HEAD_DIM].astype(BF16)


def _mla_pre(x, pos, g, win, qn, kvn, wqb, wkt, wv, inv4, sgn4, casts=()):
    tm = ATTN_TILE
    n = SEQ // tm
    return _call_with_casts(
        _mla_pre_kernel,
        (x, pos, g, win, qn, kvn, wqb, wkt, wv, inv4, sgn4),
        grid=(n,),
        in_specs=[
            pl.BlockSpec((tm, D_MODEL), lambda i: (i, 0)),
            pl.BlockSpec((tm, 1), lambda i: (i, 0)),
            _const_spec(g.shape), _const_spec(win.shape), _const_spec(qn.shape), _const_spec(kvn.shape),
            _const_spec(wqb.shape), _const_spec(wkt.shape), _const_spec(wv.shape),
            _const_spec(inv4.shape), _const_spec(sgn4.shape),
        ],
        out_specs=[
            pl.BlockSpec((HEADS, tm, QK_HEAD), lambda i: (0, i, 0)),
            pl.BlockSpec((HEADS, 1, QK_HEAD, tm), lambda i: (0, i, 0, 0)),
            pl.BlockSpec((HEADS, tm, HEAD_DIM), lambda i: (0, i, 0)),
        ],
        out_shape=[
            jax.ShapeDtypeStruct((HEADS, SEQ, QK_HEAD), BF16),
            jax.ShapeDtypeStruct((HEADS, n, QK_HEAD, tm), BF16),
            jax.ShapeDtypeStruct((HEADS, SEQ, HEAD_DIM), BF16),
        ],
        casts=casts,
        compiler_params=_params("arbitrary"),
        name="mla_pre",
    )


def _flash_kernel(q_ref, kt_hbm, v_hbm, o_ref, m_sc, l_sc, acc_sc, k_buf, v_buf, sem):
    hi = pl.program_id(0)
    qi = pl.program_id(1)
    n_h = pl.num_programs(0)
    n_q = pl.num_programs(1)
    tq, tk = Q_TILE, ATTN_TILE
    ratio = tq // tk
    ncol = tk // LANES

    def kv_copies(group, j, slot):
        hs = pl.ds(group * HEAD_GROUP, HEAD_GROUP)
        rows = pl.ds(pl.multiple_of(j * tk, tk), tk)
        return (pltpu.make_async_copy(kt_hbm.at[hs, j], k_buf.at[slot], sem.at[slot, 0]),
                pltpu.make_async_copy(v_hbm.at[hs, rows, :], v_buf.at[slot], sem.at[slot, 1]))

    def start_kv(group, j, slot):
        for c in kv_copies(group, j, slot):
            c.start()

    @pl.when(jnp.logical_and(hi == 0, qi == 0))
    def _():
        start_kv(0, 0, 0)

    m_sc[...] = jnp.full(m_sc.shape, MASK_VALUE, F32)
    l_sc[...] = jnp.zeros(l_sc.shape, F32)
    acc_sc[...] = jnp.zeros(acc_sc.shape, F32)

    def step(j, diag):
        slot = j % 2
        for c in kv_copies(hi, j, slot):
            c.wait()
        if diag is None or diag + 1 < ratio:
            start_kv(hi, j + 1, 1 - slot)
        else:
            last_q = qi + 1 == n_q

            @pl.when(jnp.logical_not(last_q))
            def _():
                start_kv(hi, 0, 0)

            @pl.when(jnp.logical_and(last_q, hi + 1 < n_h))
            def _():
                start_kv(hi + 1, 0, 0)

        rows = slice(None) if diag is None else slice(diag * tk, tq)
        for g in range(HEAD_GROUP):
            s = _dot(q_ref[g, rows, :], k_buf[slot, g])
            if diag is not None:
                row = lax.broadcasted_iota(jnp.int32, s.shape, 0)
                col = lax.broadcasted_iota(jnp.int32, s.shape, 1)
                s = jnp.where(col <= row, s, MASK_VALUE)
            cols = [s[:, c * LANES:(c + 1) * LANES] for c in range(ncol)]
            m_prev = m_sc[g, rows, :]
            m_cur = functools.reduce(jnp.maximum, cols)
            m_new = jnp.maximum(m_prev, jnp.max(m_cur, axis=-1, keepdims=True))
            alpha = jnp.exp2(m_prev - m_new)
            ps = [jnp.exp2((c - m_new).astype(BF16)) for c in cols]
            l_sc[g, rows, :] = alpha * l_sc[g, rows, :] + functools.reduce(jnp.add, ps).astype(F32)
            p = jnp.concatenate(ps, axis=-1)
            acc_sc[g, rows, :] = alpha * acc_sc[g, rows, :] + _dot(p, v_buf[slot, g])
            m_sc[g, rows, :] = m_new

    def body(j, carry):
        step(j, None)
        return carry

    lax.fori_loop(0, qi * ratio, body, 0)
    for d in range(ratio):
        step(qi * ratio + d, d)
    for g in range(HEAD_GROUP):
        l = jnp.sum(l_sc[g], axis=-1, keepdims=True)
        o_ref[:, g * HEAD_DIM:(g + 1) * HEAD_DIM] = (acc_sc[g] / l).astype(BF16)


def _flash(q, kt, v, casts=()):
    tq, tk = Q_TILE, ATTN_TILE
    hg = HEAD_GROUP
    assert (tq // tk) % 2 == 0
    assert not casts or hg == HEADS
    return _call_with_casts(
        _flash_kernel,
        (q, kt, v),
        grid=(HEADS // hg, SEQ // tq),
        in_specs=[
            pl.BlockSpec((hg, tq, QK_HEAD), lambda h, i: (h, i, 0)),
            pl.BlockSpec(memory_space=pl.ANY),
            pl.BlockSpec(memory_space=pl.ANY),
        ],
        out_specs=[pl.BlockSpec((tq, hg * HEAD_DIM), lambda h, i: (i, h))],
        out_shape=[jax.ShapeDtypeStruct((SEQ, WIDTH), BF16)],
        casts=casts,
        scratch_shapes=[
            pltpu.VMEM((hg, tq, LANES), F32),
            pltpu.VMEM((hg, tq, LANES), F32),
            pltpu.VMEM((hg, tq, HEAD_DIM), F32),
            pltpu.VMEM((2, hg, QK_HEAD, tk), BF16),
            pltpu.VMEM((2, hg, tk, HEAD_DIM), BF16),
            pltpu.SemaphoreType.DMA((2, 2)),
        ],
        compiler_params=_params("arbitrary", "arbitrary"),
        name="mla_flash",
    )


def _gelu(x):
    return jax.nn.gelu(x)


def _gmlp_kernel(x_ref, g_ref, wu_ref, wv_ref, lng_ref, lnb_ref, ws_ref, bst_ref, o_ref):
    tm = x_ref.shape[0]
    h = _rms(x_ref[...], g_ref[...]).astype(BF16)
    u = _gelu(_dot(h, wu_ref[...]))
    gv = _gelu(_dot(h, wv_ref[...]))
    mu = jnp.mean(gv, axis=-1, keepdims=True)
    var = jnp.mean(jnp.square(gv - mu), axis=-1, keepdims=True)
    vn = ((gv - mu) * lax.rsqrt(var + EPS) * lng_ref[...] + lnb_ref[...]).astype(BF16)
    row = lax.broadcasted_iota(jnp.int32, (CHUNK, CHUNK), 0)
    col = lax.broadcasted_iota(jnp.int32, (CHUNK, CHUNK), 1)
    bst = bst_ref[...]
    for hd in range(HEADS):
        w = jnp.where(col <= row, ws_ref[hd], 0.0).astype(BF16)
        bias = bst[:, hd:hd + 1]
        cs = slice(hd * HEAD_DIM, (hd + 1) * HEAD_DIM)
        for c in range(tm // CHUNK):
            rs = slice(c * CHUNK, (c + 1) * CHUNK)
            gate = _dot(w, vn[rs, cs]) + bias
            o_ref[rs, cs] = (u[rs, cs] * gate).astype(BF16)


def _gmlp(x, g, wu, wv, lng, lnb, ws, bst, casts=()):
    tm = ROW_TILE
    return _call_with_casts(
        _gmlp_kernel,
        (x, g, wu, wv, lng, lnb, ws, bst),
        grid=(SEQ // tm,),
        in_specs=[
            pl.BlockSpec((tm, D_MODEL), lambda i: (i, 0)),
            _const_spec(g.shape), _const_spec(wu.shape), _const_spec(wv.shape),
            _const_spec(lng.shape), _const_spec(lnb.shape), _const_spec(ws.shape), _const_spec(bst.shape),
        ],
        out_specs=[pl.BlockSpec((tm, WIDTH), lambda i: (i, 0))],
        out_shape=[jax.ShapeDtypeStruct((SEQ, WIDTH), BF16)],
        casts=casts,
        compiler_params=_params("arbitrary"),
        name="gmlp",
    )


def _outproj_kernel(x_ref, a_ref, b_ref, wa_ref, wb_ref, o_ref):
    o_ref[...] = x_ref[...] + _dot(a_ref[...], wa_ref[...]) + _dot(b_ref[...], wb_ref[...])


def _outproj(x, a, b, w, casts=()):
    tm = ROW_TILE
    return _call_with_casts(
        _outproj_kernel,
        (x, a, b, w, w),
        grid=(SEQ // tm,),
        in_specs=[
            pl.BlockSpec((tm, D_MODEL), lambda i: (i, 0)),
            pl.BlockSpec((tm, WIDTH), lambda i: (i, 0)),
            pl.BlockSpec((tm, WIDTH), lambda i: (i, 0)),
            pl.BlockSpec((WIDTH, D_MODEL), lambda i: (0, 0), pipeline_mode=pl.Buffered(1)),
            pl.BlockSpec((WIDTH, D_MODEL), lambda i: (1, 0), pipeline_mode=pl.Buffered(1)),
        ],
        out_specs=[pl.BlockSpec((tm, D_MODEL), lambda i: (i, 0))],
        out_shape=[jax.ShapeDtypeStruct((SEQ, D_MODEL), F32)],
        casts=casts,
        compiler_params=_params("arbitrary"),
        name="outproj0",
    )


def _ffn_kernel(x_ref, g_ref, wg_ref, wu_ref, wd_ref, o_ref, h_sc):
    @pl.when(pl.program_id(1) == 0)
    def _():
        x = x_ref[...]
        h_sc[...] = _rms(x, g_ref[...]).astype(BF16)
        o_ref[...] = x

    h = h_sc[...]
    half = wg_ref.shape[1] // 2
    acc = None
    for s in range(2):
        cs = slice(s * half, (s + 1) * half)
        a = _dot(h, wg_ref[:, cs])
        b = _dot(h, wu_ref[:, cs])
        t = (a * jax.nn.sigmoid(a) * b).astype(BF16)
        part = _dot(t, wd_ref[cs, :])
        acc = part if acc is None else acc + part
    o_ref[...] += acc


def _ffn(x, g, wg, wu, wd):
    tm, tf = FFN_ROW_TILE, FF_TILE
    return pl.pallas_call(
        _ffn_kernel,
        grid=(SEQ // tm, D_FF // tf),
        in_specs=[
            pl.BlockSpec((tm, D_MODEL), lambda i, j: (i, 0)),
            _const_spec(g.shape),
            pl.BlockSpec((D_MODEL, tf), lambda i, j: (0, j)),
            pl.BlockSpec((D_MODEL, tf), lambda i, j: (0, j)),
            pl.BlockSpec((tf, D_MODEL), lambda i, j: (j, 0)),
        ],
        out_specs=pl.BlockSpec((tm, D_MODEL), lambda i, j: (i, 0)),
        out_shape=jax.ShapeDtypeStruct((SEQ, D_MODEL), F32),
        scratch_shapes=[pltpu.VMEM((tm, D_MODEL), BF16)],
        compiler_params=_params("arbitrary", "arbitrary"),
        name="ffn",
    )(x, g, wg, wu, wd)


def _conv_kernel(x_ref, g_ref, wb_ref, wc_ref, wh_ref, cw_ref, wo_ref, p_ref, gp_ref, wgp_ref, wpp_ref,
                 o_ref, h_sc, tail_sc):
    i = pl.program_id(0)
    j = pl.program_id(1)

    @pl.when(j == 0)
    def _():
        x = x_ref[...]
        hp = _rms(x, gp_ref[...]).astype(BF16)
        gate = jax.nn.sigmoid(_dot(hp, wgp_ref[...]))
        x = x + gate * _dot(p_ref[...].astype(BF16), wpp_ref[...])
        h_sc[...] = _rms(x, g_ref[...]).astype(BF16)
        o_ref[...] = x

    @pl.when(i == 0)
    def _():
        tail_sc[j] = jnp.zeros(tail_sc.shape[1:], F32)

    h = h_sc[...]
    tail = tail_sc[j]
    cw = cw_ref[...]
    half = wb_ref.shape[1] // 2
    acc = None
    for s in range(2):
        cs = slice(s * half, (s + 1) * half)
        gb = _dot(h, wb_ref[:, cs])
        cz = _dot(h, wc_ref[:, cs]) * _dot(h, wh_ref[:, cs])
        prev1 = tail[7:8, cs]
        prev2 = tail[6:7, cs]
        row = lax.broadcasted_iota(jnp.int32, cz.shape, 0)
        z1 = jnp.where(row == 0, prev1, pltpu.roll(cz, 1, axis=0))
        z2 = jnp.where(row == 0, prev2, jnp.where(row == 1, prev1, pltpu.roll(cz, 2, axis=0)))
        y = cw[0:1, cs] * z2 + cw[1:2, cs] * z1 + cw[2:3, cs] * cz
        tail_sc[j, :, cs] = cz[cz.shape[0] - 8:, :]
        part = _dot((gb * y).astype(BF16), wo_ref[cs, :])
        acc = part if acc is None else acc + part
    o_ref[...] += acc


def _conv_mixer(x, g, win, cw, wo, p, gp, wgp, wpp, casts=()):
    tm, tc = ROW_TILE, FF_TILE
    nc = D_MODEL // tc
    return _call_with_casts(
        _conv_kernel,
        (x, g, win, win, win, cw, wo, p, gp, wgp, wpp),
        grid=(SEQ // tm, nc),
        in_specs=[
            pl.BlockSpec((tm, D_MODEL), lambda i, j: (i, 0)),
            _const_spec(g.shape),
            pl.BlockSpec((D_MODEL, tc), lambda i, j: (0, j)),
            pl.BlockSpec((D_MODEL, tc), lambda i, j: (0, j + nc)),
            pl.BlockSpec((D_MODEL, tc), lambda i, j: (0, j + 2 * nc)),
            pl.BlockSpec((8, tc), lambda i, j: (0, j)),
            pl.BlockSpec((tc, D_MODEL), lambda i, j: (j, 0)),
            pl.BlockSpec((None, None, tm, PLE_DIM), lambda i, j: (0, 0, i, 0)),
            _const_spec(gp.shape),
            pl.BlockSpec((None, D_MODEL, D_MODEL), lambda i, j: (0, 0, 0), pipeline_mode=pl.Buffered(1)),
            pl.BlockSpec((None, PLE_DIM, D_MODEL), lambda i, j: (0, 0, 0), pipeline_mode=pl.Buffered(1)),
        ],
        out_specs=[pl.BlockSpec((tm, D_MODEL), lambda i, j: (i, 0))],
        out_shape=[jax.ShapeDtypeStruct((SEQ, D_MODEL), F32)],
        casts=casts,
        scratch_shapes=[pltpu.VMEM((tm, D_MODEL), BF16), pltpu.VMEM((nc, 8, tc), F32)],
        compiler_params=_params("arbitrary", "arbitrary"),
        name="conv_mixer",
    )


def _ple_kernel(x_ref, p_ref, g_ref, wg_ref, wp_ref, gf_ref, o_ref, *, final_norm):
    x = x_ref[...]
    h = _rms(x, g_ref[...]).astype(BF16)
    gate = jax.nn.sigmoid(_dot(h, wg_ref[...]))
    y = x + gate * _dot(p_ref[...].astype(BF16), wp_ref[...])
    if final_norm:
        y = _rms(y, gf_ref[...])
    o_ref[...] = y


def _ple(x, p, g, wg, wp, gf, layer, final_norm):
    tm = PLE_ROW_TILE
    return pl.pallas_call(
        functools.partial(_ple_kernel, final_norm=final_norm),
        grid=(SEQ // tm,),
        in_specs=[
            pl.BlockSpec((tm, D_MODEL), lambda i: (i, 0)),
            pl.BlockSpec((None, None, tm, PLE_DIM), lambda i: (layer, 0, i, 0)),
            _const_spec(g.shape),
            pl.BlockSpec((None, D_MODEL, D_MODEL), lambda i: (layer, 0, 0), pipeline_mode=pl.Buffered(1)),
            pl.BlockSpec((None, PLE_DIM, D_MODEL), lambda i: (layer, 0, 0), pipeline_mode=pl.Buffered(1)),
            _const_spec(gf.shape),
        ],
        out_specs=pl.BlockSpec((tm, D_MODEL), lambda i: (i, 0)),
        out_shape=jax.ShapeDtypeStruct((SEQ, D_MODEL), F32),
        compiler_params=_params("arbitrary"),
        name="ple",
    )(x, p, g, wg, wp, gf)


def _row(v):
    return v.reshape(1, -1)


def kernel(x, p, positions, norm_mix, norm_ffn, norm_ple, w_in0, q_norm, kv_norm, w_qb, w_kvb, v_ln_g,
           v_ln_b, w_spatial, b_spatial, w_out0, w_in1, conv_w, w_out1, w_gate, w_up, w_down, w_ple_gate,
           w_ple_proj, norm_final):
    assert x.shape == (1, SEQ, D_MODEL) and p.shape == (2, 1, SEQ, PLE_DIM)
    xs = x.reshape(SEQ, D_MODEL)
    pos = positions.reshape(SEQ, 1)

    o1, o2, o3 = Q_LORA, Q_LORA + KV_LORA, Q_LORA + KV_LORA + QK_ROPE
    w0 = w_in0[0]
    k1, k2 = w0[:, o2:o2 + ROPE_HALF], w0[:, o2 + ROPE_HALF:o3]
    win_mla = jnp.concatenate([w0[:, :o2], k1, k2, k2, k1], axis=1).astype(BF16)
    wu0 = w0[:, o3:o3 + WIDTH].astype(BF16)
    wv0 = w0[:, o3 + WIDTH:].astype(BF16)
    wq = w_qb[0].reshape(Q_LORA, HEADS, QK_HEAD)
    pe = wq[:, :, HEAD_DIM:]
    pe_swapped = jnp.concatenate([pe[:, :, ROPE_HALF:], pe[:, :, :ROPE_HALF]], axis=-1)
    wqb = jnp.concatenate([wq[:, :, :HEAD_DIM].reshape(Q_LORA, WIDTH), pe.reshape(Q_LORA, HEADS * QK_ROPE),
                           pe_swapped.reshape(Q_LORA, HEADS * QK_ROPE)], axis=1).astype(BF16)
    wkv = w_kvb[0].reshape(KV_LORA, HEADS, 2 * HEAD_DIM)
    wkt = wkv[:, :, :HEAD_DIM].reshape(KV_LORA, WIDTH).T.astype(BF16)
    wv = wkv[:, :, HEAD_DIM:].reshape(KV_LORA, WIDTH).astype(BF16)
    inv_freq = ROPE_BASE ** (-jnp.arange(ROPE_HALF, dtype=F32) / ROPE_HALF)
    inv4 = jnp.tile(inv_freq, LANES // ROPE_HALF).reshape(1, LANES)
    sgn4 = jnp.tile(jnp.concatenate([-jnp.ones(ROPE_HALF, F32), jnp.ones(ROPE_HALF, F32)]), 2).reshape(1, LANES)
    cw = jnp.concatenate([conv_w[0], jnp.zeros((8 - CONV_W, D_MODEL), F32)], axis=0)
    wp_ple = w_ple_proj.astype(BF16)

    n_attn, n_row = SEQ // ATTN_TILE, SEQ // ROW_TILE

    def slab_cast(w, layer, steps):
        rows, cols = w.shape[1] // steps, w.shape[2]
        return _Cast(w, pl.BlockSpec((None, rows, cols), lambda i: (layer, i, 0)),
                     pl.BlockSpec((rows, cols), lambda i: (i, 0)),
                     jax.ShapeDtypeStruct(w.shape[1:], BF16))

    def tile_cast(w, layer):
        blk = (w.shape[1] // n_row, w.shape[2] // (D_MODEL // FF_TILE))
        return _Cast(w, pl.BlockSpec((None,) + blk, lambda i, j: (layer, i, j)),
                     pl.BlockSpec(blk, lambda i, j: (i, j)), jax.ShapeDtypeStruct(w.shape[1:], BF16))

    n_q = SEQ // Q_TILE

    def q_slab_cast(w, layers):
        rows, cols = w.shape[1] // n_q, w.shape[2]
        spec = pl.BlockSpec((layers, rows, cols), lambda h, i: (0, i, 0))
        return _Cast(w, spec, spec, jax.ShapeDtypeStruct(w.shape, BF16))

    (q, kt, v), (wg0, wu0_ffn) = _mla_pre(
        xs, pos, _row(norm_mix[0]), win_mla, _row(q_norm[0]), _row(kv_norm[0]), wqb, wkt, wv, inv4, sgn4,
        casts=[slab_cast(w_gate, 0, n_attn), slab_cast(w_up, 0, n_attn)])
    (attn,), (wo0, wo1, wg_ple) = _flash(
        q, kt, v, casts=[q_slab_cast(w_out0, 1), q_slab_cast(w_out1, 1), q_slab_cast(w_ple_gate, 2)])
    wo0, wo1 = wo0[0], wo1[0]
    (gm,), (wd0, win1) = _gmlp(
        xs, _row(norm_mix[0]), wu0, wv0, _row(v_ln_g[0]), _row(v_ln_b[0]), w_spatial[0], b_spatial[0].T,
        casts=[slab_cast(w_down, 0, n_row), slab_cast(w_in1, 0, n_row)])
    (xs,), _ = _outproj(xs, attn, gm, wo0)
    xs = _ffn(xs, _row(norm_ffn[0]), wg0, wu0_ffn, wd0)

    (xs,), (wg1, wu1, wd1) = _conv_mixer(
        xs, _row(norm_mix[1]), win1, cw, wo1, p, _row(norm_ple[0]), wg_ple, wp_ple,
        casts=[tile_cast(w_gate, 1), tile_cast(w_up, 1), tile_cast(w_down, 1)])
    xs = _ffn(xs, _row(norm_ffn[1]), wg1, wu1, wd1)
    xs = _ple(xs, p, _row(norm_ple[1]), wg_ple, wp_ple, _row(norm_final), layer=1, final_norm=True)
    return xs.reshape(1, SEQ, D_MODEL)
```
